```python
import math
import jax, jax.numpy as jnp
from jax import lax
import numpy as np

D_MODEL = 2048
BATCH = 2
SEQ = 4096
DEPTH = 2
DEC_BATCH = 4
DEC_SEQ = 2048
PAST_LEN = 128

N_MIXERS = 2
N_POOL_LAYERS = (DEPTH + 1) // 2
N_ATTN_LAYERS = DEPTH // 2
POOL_WINDOWS = (2, 4, 8, 16)
N_POOL_GROUPS = len(POOL_WINDOWS)
POOL_GC = D_MODEL // N_POOL_GROUPS
HEAD_DIM = 128
N_HEADS = D_MODEL // HEAD_DIM
N_KV_HEADS = N_HEADS // 4
GQA_GROUP = N_HEADS // N_KV_HEADS
QKV_DIM = (N_HEADS + 2 * N_KV_HEADS) * HEAD_DIM
AXIS_DIM = HEAD_DIM // 2
ROPE_THETA = 10000.0
GRID_W = 64
Q_BLOCK = 128
D_FF = 5632
N_MOD = 9
EPS = 1e-6

kernel_name = "hybrid_pool_gqa_macaron_adaln_encoder"


def _rmsnorm(x, g):
    xf = x.astype(jnp.float32)
    y = xf * lax.rsqrt(jnp.mean(xf * xf, axis=-1, keepdims=True) + EPS)
    return (y * g.astype(jnp.float32)).astype(x.dtype)


def _swiglu(h, w_in, w_out):
    gu = h @ w_in
    g, u = jnp.split(gu, 2, axis=-1)
    return (jax.nn.silu(g) * u) @ w_out


def _pool_mixer(h, w_grp, scale):
    B, S, D = h.shape
    hf = h.astype(jnp.float32)
    cs = jnp.concatenate([jnp.zeros((B, 1, D), jnp.float32), jnp.cumsum(hf, axis=1)], axis=1)
    t = jnp.arange(S)
    outs = []
    for gi, w in enumerate(POOL_WINDOWS):
        lo = jnp.clip(t - w // 2, 0, S)
        hi = jnp.clip(t + w // 2, 0, S)
        csg = cs[..., gi * POOL_GC:(gi + 1) * POOL_GC]
        win_sum = jnp.take(csg, hi, axis=1) - jnp.take(csg, lo, axis=1)
        cnt = (hi - lo).astype(jnp.float32)[None, :, None]
        outs.append(win_sum / cnt - hf[..., gi * POOL_GC:(gi + 1) * POOL_GC])
    p = jnp.stack(outs, axis=2).astype(h.dtype)
    y = jnp.einsum('bsgc,gcd->bsgd', p, w_grp).reshape(B, S, D)
    return y * scale


def _axial_rope_tables(S):
    rows = S // GRID_W
    r = jnp.repeat(jnp.arange(rows), GRID_W).astype(jnp.float32)
    c = jnp.tile(jnp.arange(GRID_W), rows).astype(jnp.float32)
    inv = ROPE_THETA ** (-jnp.arange(0, AXIS_DIM, 2, dtype=jnp.float32) / AXIS_DIM)
    ang_r = r[:, None] * inv[None, :]
    ang_c = c[:, None] * inv[None, :]
    return jnp.cos(ang_r), jnp.sin(ang_r), jnp.cos(ang_c), jnp.sin(ang_c)


def _rotate(x, cos, sin):
    x1, x2 = jnp.split(x, 2, axis=-1)
    cs = cos[None, :, None, :]
    sn = sin[None, :, None, :]
    return jnp.concatenate([x1 * cs - x2 * sn, x1 * sn + x2 * cs], axis=-1)


def _apply_axial_rope(x, tables):
    cr, sr, cc, sc = tables
    xf = x.astype(jnp.float32)
    xr = _rotate(xf[..., :AXIS_DIM], cr, sr)
    xc = _rotate(xf[..., AXIS_DIM:], cc, sc)
    return jnp.concatenate([xr, xc], axis=-1).astype(x.dtype)


def _attn_mixer(h, w_qkv, q_g, k_g, w_o):
    B, S, D = h.shape
    qkv = h @ w_qkv
    q = qkv[..., :N_HEADS * HEAD_DIM].reshape(B, S, N_HEADS, HEAD_DIM)
    k = qkv[..., N_HEADS * HEAD_DIM:(N_HEADS + N_KV_HEADS) * HEAD_DIM].reshape(B, S, N_KV_HEADS, HEAD_DIM)
    v = qkv[..., (N_HEADS + N_KV_HEADS) * HEAD_DIM:].reshape(B, S, N_KV_HEADS, HEAD_DIM)
    q = _rmsnorm(q, q_g)
    k = _rmsnorm(k, k_g)
    tables = _axial_rope_tables(S)
    q = _apply_axial_rope(q, tables)
    k = _apply_axial_rope(k, tables)
    n_blk = S // Q_BLOCK
    qb = q.reshape(B, n_blk, Q_BLOCK, N_KV_HEADS, GQA_GROUP, HEAD_DIM).transpose(1, 0, 2, 3, 4, 5)
    scale = 1.0 / math.sqrt(HEAD_DIM)

    def one_block(qi):
        s = jnp.einsum('bqkgd,bskd->bkgqs', qi, k, preferred_element_type=jnp.float32) * scale
        p = jax.nn.softmax(s, axis=-1)
        return jnp.einsum('bkgqs,bskd->bqkgd', p.astype(v.dtype), v)

    o = lax.map(one_block, qb)
    o = o.transpose(1, 0, 2, 3, 4, 5).reshape(B, S, N_HEADS * HEAD_DIM)
    return o @ w_o


def _trunk(x, c, ada_w, ada_b, norm_g, ffn_w_in, ffn_w_out, pool_w, pool_scale,
           attn_w_qkv, attn_q_g, attn_k_g, attn_w_o):
    B, S, D = x.shape
    for i in range(DEPTH):
        mod = (jax.nn.silu(c) @ ada_w[i] + ada_b[i]).reshape(B, 3, 3, D)
        shift = mod[:, :, 0, None, :]
        scl = mod[:, :, 1, None, :]
        gate = mod[:, :, 2, None, :]
        h = _rmsnorm(x, norm_g[i, 0]) * (1.0 + scl[:, 0]) + shift[:, 0]
        x = x + 0.5 * gate[:, 0] * _swiglu(h, ffn_w_in[i, 0], ffn_w_out[i, 0])
        h = _rmsnorm(x, norm_g[i, 1]) * (1.0 + scl[:, 1]) + shift[:, 1]
        j = i // N_MIXERS
        if i % N_MIXERS == 0:
            m = _pool_mixer(h, pool_w[j], pool_scale[j])
        else:
            m = _attn_mixer(h, attn_w_qkv[j], attn_q_g[j], attn_k_g[j], attn_w_o[j])
        x = x + gate[:, 1] * m
        h = _rmsnorm(x, norm_g[i, 2]) * (1.0 + scl[:, 2]) + shift[:, 2]
        x = x + 0.5 * gate[:, 2] * _swiglu(h, ffn_w_in[i, 1], ffn_w_out[i, 1])
    return x


def setup_inputs(seed: int = 0) -> dict:
    key = jax.random.key(seed)
    ks = jax.random.split(key, 16)
    f32 = jnp.float32
    nrm = lambda k, shape, s: jax.random.normal(k, shape, f32) * s
    return {
        "x_prompt": nrm(ks[0], (BATCH, SEQ, D_MODEL), 1.0),
        "x_sample": nrm(ks[1], (DEC_BATCH, DEC_SEQ, D_MODEL), 1.0),
        "c_prompt": nrm(ks[2], (BATCH, D_MODEL), 1.0),
        "c_sample": nrm(ks[3], (DEC_BATCH, D_MODEL), 1.0),
        "ada_w": nrm(ks[4], (DEPTH, D_MODEL, N_MOD * D_MODEL), 0.5 * D_MODEL ** -0.5),
        "ada_b": nrm(ks[5], (DEPTH, N_MOD * D_MODEL), 0.02),
        "norm_g": 1.0 + nrm(ks[6], (DEPTH, 3, D_MODEL), 0.05),
        "ffn_w_in": nrm(ks[7], (DEPTH, 2, D_MODEL, 2 * D_FF), D_MODEL ** -0.5),
        "ffn_w_out": nrm(ks[8], (DEPTH, 2, D_FF, D_MODEL), D_FF ** -0.5),
        "pool_w": nrm(ks[9], (N_POOL_LAYERS, N_POOL_GROUPS, POOL_GC, POOL_GC), POOL_GC ** -0.5),
        "pool_scale": 1.0 + nrm(ks[10], (N_POOL_LAYERS, D_MODEL), 0.1),
        "attn_w_qkv": nrm(ks[11], (N_ATTN_LAYERS, D_MODEL, QKV_DIM), D_MODEL ** -0.5),
        "attn_q_g": 1.0 + nrm(ks[12], (N_ATTN_LAYERS, HEAD_DIM), 0.05),
        "attn_k_g": 1.0 + nrm(ks[13], (N_ATTN_LAYERS, HEAD_DIM), 0.05),
        "attn_w_o": nrm(ks[14], (N_ATTN_LAYERS, N_HEADS * HEAD_DIM, D_MODEL), (N_HEADS * HEAD_DIM) ** -0.5),
    }


def reference(x_prompt, x_sample, c_prompt, c_sample, ada_w, ada_b, norm_g, ffn_w_in, ffn_w_out,
              pool_w, pool_scale, attn_w_qkv, attn_q_g, attn_k_g, attn_w_o):
    y_prompt = _trunk(x_prompt, c_prompt, ada_w, ada_b, norm_g, ffn_w_in, ffn_w_out, pool_w, pool_scale,
                      attn_w_qkv, attn_q_g, attn_k_g, attn_w_o)
    y_sample = _trunk(x_sample, c_sample, ada_w, ada_b, norm_g, ffn_w_in, ffn_w_out, pool_w, pool_scale,
                      attn_w_qkv, attn_q_g, attn_k_g, attn_w_o)
    return (y_prompt, y_sample)
```

```python
import functools
import math

import jax
import jax.numpy as jnp
from jax import lax
from jax.experimental import pallas as pl
from jax.experimental.pallas import tpu as pltpu

F32 = jnp.float32
BF16 = jnp.bfloat16

EPS = 1e-6
HEAD_DIM = 128
AXIS_DIM = HEAD_DIM // 2
ROPE_THETA = 10000.0
GRID_W = 64
GQA_GROUP = 4
POOL_WINDOWS = (2, 4, 8, 16)
POOL_HALO = 8
N_MOD_ROWS = 9

VMEM_LIMIT_BYTES = 60 * 1024 * 1024


def _params(semantics):
    return pltpu.CompilerParams(dimension_semantics=semantics, vmem_limit_bytes=VMEM_LIMIT_BYTES)


def _mod_norm(x, g, shift, scl):
    ms = jnp.mean(x * x, axis=-1, keepdims=True)
    y = x * lax.rsqrt(ms + EPS)
    return (y * g) * (1.0 + scl) + shift


def _ada_kernel(c_ref, w_ref, b_ref, o_ref):
    c = c_ref[...]
    s = (c * jax.nn.sigmoid(c)).astype(BF16)
    w = w_ref[0].astype(BF16)
    o_ref[0] = jnp.dot(s, w, preferred_element_type=F32) + b_ref[0]


def _ada_mod(c_seg, ada_w, ada_b, tn=1024):
    depth, d, n = ada_w.shape
    nseg = c_seg.shape[0]
    return pl.pallas_call(
        _ada_kernel,
        grid=(depth, n // tn),
        in_specs=[
            pl.BlockSpec((nseg, d), lambda l, j: (0, 0)),
            pl.BlockSpec((1, d, tn), lambda l, j: (l, 0, j)),
            pl.BlockSpec((1, 1, tn), lambda l, j: (l, 0, j)),
        ],
        out_specs=pl.BlockSpec((1, nseg, tn), lambda l, j: (l, 0, j)),
        out_shape=jax.ShapeDtypeStruct((depth, nseg, n), F32),
        compiler_params=_params(("arbitrary", "arbitrary")),
        name="ada_mod",
    )(c_seg, ada_w, ada_b.reshape(depth, 1, n))


def _mod_specs(tm, seg_rows, sub, d, ngrid):
    specs = []
    for r in range(3):
        row = sub * 3 + r
        if ngrid == 1:
            imap = lambda i, row=row: ((i * tm) // seg_rows * N_MOD_ROWS + row, 0, 0)
        else:
            imap = lambda i, k, row=row: ((i * tm) // seg_rows * N_MOD_ROWS + row, 0, 0)
        specs.append(pl.BlockSpec((1, 1, d), imap))
    return specs


def _ffn_kernel(x_ref, g_ref, sh_ref, sc_ref, gt_ref, wg_ref, wu_ref, wo_ref, o_ref, h_ref):
    k = pl.program_id(1)

    @pl.when(k == 0)
    def _():
        h = _mod_norm(x_ref[...], g_ref[...], sh_ref[0], sc_ref[0])
        h_ref[...] = h.astype(BF16)
        o_ref[...] = jnp.zeros_like(o_ref)

    hb = h_ref[...]
    g = jnp.dot(hb, wg_ref[...], preferred_element_type=F32)
    u = jnp.dot(hb, wu_ref[...], preferred_element_type=F32)
    a = (g * jax.nn.sigmoid(g) * u).astype(BF16)
    tn = wg_ref.shape[1]
    for c in range(o_ref.shape[1] // tn):
        cols = slice(c * tn, (c + 1) * tn)
        o_ref[:, cols] += jnp.dot(a, wo_ref[:, cols], preferred_element_type=F32)

    @pl.when(k == pl.num_programs(1) - 1)
    def _():
        o_ref[...] = x_ref[...] + (0.5 * gt_ref[0]) * o_ref[...]


def _ffn(x, gain, mod, sub, w_in, w_out, li, fi, seg_rows, tm=512, tf=512):
    t, d = x.shape
    d_ff = w_out.shape[2]
    nk = d_ff // tf
    return pl.pallas_call(
        _ffn_kernel,
        grid=(t // tm, nk),
        in_specs=[
            pl.BlockSpec((tm, d), lambda i, k: (i, 0)),
            pl.BlockSpec((1, d), lambda i, k: (0, 0)),
            *_mod_specs(tm, seg_rows, sub, d, 2),
            pl.BlockSpec((None, None, d, tf), lambda i, k: (li, fi, 0, k)),
            pl.BlockSpec((None, None, d, tf), lambda i, k: (li, fi, 0, k + nk)),
            pl.BlockSpec((None, None, tf, d), lambda i, k: (li, fi, k, 0)),
        ],
        out_specs=pl.BlockSpec((tm, d), lambda i, k: (i, 0)),
        out_shape=jax.ShapeDtypeStruct((t, d), F32),
        scratch_shapes=[pltpu.VMEM((tm, d), BF16)],
        compiler_params=_params(("arbitrary", "arbitrary")),
        name="ffn",
    )(x, gain.reshape(1, d), mod, mod, mod, w_in, w_in, w_out)


def _pool_kernel(x_ref, xp_ref, xn_ref, g_ref, sh_ref, sc_ref, gt_ref, w_ref, ps_ref, o_ref, hbuf,
                 *, tp, seq_lens, group_rows):
    i = pl.program_id(0)
    row0 = i * tp
    in_first = row0 < group_rows
    slen = jnp.where(in_first, seq_lens[0], seq_lens[1])
    pos0 = jnp.where(in_first, row0 % seq_lens[0], (row0 - group_rows) % seq_lens[1])
    g = g_ref[...]
    sh = sh_ref[0]
    sc = sc_ref[0]
    x = x_ref[...]
    d = x.shape[-1]
    gc = d // len(POOL_WINDOWS)

    has_prev = pos0 > 0
    has_next = pos0 + tp < slen
    hp = _mod_norm(xp_ref[...], g, sh, sc)
    hn = _mod_norm(xn_ref[...], g, sh, sc)
    hbuf[0:POOL_HALO, :] = jnp.where(has_prev, hp, 0.0)
    hbuf[POOL_HALO:POOL_HALO + tp, :] = _mod_norm(x, g, sh, sc)
    hbuf[POOL_HALO + tp:, :] = jnp.where(has_next, hn, 0.0)

    pos = pos0 + lax.broadcasted_iota(jnp.int32, (tp, 1), 0)
    outs = []
    for gi, w in enumerate(POOL_WINDOWS):
        cols = slice(gi * gc, (gi + 1) * gc)
        acc = hbuf[POOL_HALO - w // 2:POOL_HALO - w // 2 + tp, cols]
        for j in range(-w // 2 + 1, w // 2):
            acc = acc + hbuf[POOL_HALO + j:POOL_HALO + j + tp, cols]
        lo = jnp.maximum(pos - w // 2, 0)
        hi = jnp.minimum(pos + w // 2, slen)
        cnt = (hi - lo).astype(F32)
        p = acc / cnt - hbuf[POOL_HALO:POOL_HALO + tp, cols]
        outs.append(jnp.dot(p.astype(BF16), w_ref[gi], preferred_element_type=F32))
    m = jnp.concatenate(outs, axis=-1) * ps_ref[...]
    o_ref[...] = x + gt_ref[0] * m


def _pool(x, gain, mod, sub, pool_w, pool_scale, seg_rows, seq_lens, group_rows, tp=256):
    t, d = x.shape
    hb = tp // POOL_HALO
    nhalo = t // POOL_HALO
    kern = functools.partial(_pool_kernel, tp=tp, seq_lens=seq_lens, group_rows=group_rows)
    return pl.pallas_call(
        kern,
        grid=(t // tp,),
        in_specs=[
            pl.BlockSpec((tp, d), lambda i: (i, 0)),
            pl.BlockSpec((POOL_HALO, d), lambda i: (jnp.maximum(i * hb - 1, 0), 0)),
            pl.BlockSpec((POOL_HALO, d), lambda i: (jnp.minimum((i + 1) * hb, nhalo - 1), 0)),
            pl.BlockSpec((1, d), lambda i: (0, 0)),
            *_mod_specs(tp, seg_rows, sub, d, 1),
            pl.BlockSpec(pool_w.shape, lambda i: (0, 0, 0)),
            pl.BlockSpec((1, d), lambda i: (0, 0)),
        ],
        out_specs=pl.BlockSpec((tp, d), lambda i: (i, 0)),
        out_shape=jax.ShapeDtypeStruct((t, d), F32),
        scratch_shapes=[pltpu.VMEM((tp + 2 * POOL_HALO, d), F32)],
        compiler_params=_params(("arbitrary",)),
        name="pool_mixer",
    )(x, x, x, gain.reshape(1, d), mod, mod, mod, pool_w, pool_scale.reshape(1, d))


def _rope_tables(s):
    t = jnp.arange(s)
    r = (t // GRID_W).astype(F32)
    c = (t % GRID_W).astype(F32)
    inv = ROPE_THETA ** (-jnp.arange(0, AXIS_DIM, 2, dtype=F32) / AXIS_DIM)
    ang_r = r[:, None] * inv[None, :]
    ang_c = c[:, None] * inv[None, :]
    cos = jnp.concatenate([jnp.cos(ang_r)] * 2 + [jnp.cos(ang_c)] * 2, axis=-1)
    sin = jnp.concatenate([-jnp.sin(ang_r), jnp.sin(ang_r), -jnp.sin(ang_c), jnp.sin(ang_c)], axis=-1)
    return cos, sin


def _head_norm_rope(xh, gain, cos, sin, first_half):
    ms = jnp.mean(xh * xh, axis=-1, keepdims=True)
    y = xh * lax.rsqrt(ms + EPS) * gain
    half = AXIS_DIM // 2
    partner = jnp.where(first_half, pltpu.roll(y, HEAD_DIM - half, 1), pltpu.roll(y, half, 1))
    return y * cos + partner * sin


def _qkv_kernel(x_ref, g_ref, sh_ref, sc_ref, w_ref, qg_ref, kg_ref, cos_ref, sin_ref,
                q_ref, k_ref, v_ref, *, n_heads, n_kv):
    h = _mod_norm(x_ref[...], g_ref[...], sh_ref[0], sc_ref[0]).astype(BF16)
    qkv = jnp.dot(h, w_ref[...], preferred_element_type=F32)
    cos = cos_ref[...]
    sin = sin_ref[...]
    lane = lax.broadcasted_iota(jnp.int32, cos.shape, 1)
    first_half = (lane % AXIS_DIM) < (AXIS_DIM // 2)
    qg = qg_ref[...]
    kg = kg_ref[...]
    for hd in range(n_heads):
        cols = slice(hd * HEAD_DIM, (hd + 1) * HEAD_DIM)
        q_ref[:, cols] = _head_norm_rope(qkv[:, cols], qg, cos, sin, first_half).astype(BF16)
    k0 = n_heads * HEAD_DIM
    for hd in range(n_kv):
        cols = slice(k0 + hd * HEAD_DIM, k0 + (hd + 1) * HEAD_DIM)
        k_ref[:, hd * HEAD_DIM:(hd + 1) * HEAD_DIM] = _head_norm_rope(
            qkv[:, cols], kg, cos, sin, first_half).astype(BF16)
    v_ref[...] = qkv[:, k0 + n_kv * HEAD_DIM:].astype(BF16)


def _qkv(x, gain, mod, sub, w_qkv, q_g, k_g, cos, sin, seg_rows, seq_lens, group_rows, tm=512):
    t, d = x.shape
    n_tot = w_qkv.shape[1] // HEAD_DIM
    n_heads = d // HEAD_DIM
    n_kv = (n_tot - n_heads) // 2

    def pos_block(i):
        row0 = i * tm
        pos0 = jnp.where(row0 < group_rows, row0 % seq_lens[0], (row0 - group_rows) % seq_lens[1])
        return (pos0 // tm, 0)

    kern = functools.partial(_qkv_kernel, n_heads=n_heads, n_kv=n_kv)
    return pl.pallas_call(
        kern,
        grid=(t // tm,),
        in_specs=[
            pl.BlockSpec((tm, d), lambda i: (i, 0)),
            pl.BlockSpec((1, d), lambda i: (0, 0)),
            *_mod_specs(tm, seg_rows, sub, d, 1)[:2],
            pl.BlockSpec(w_qkv.shape, lambda i: (0, 0)),
            pl.BlockSpec((1, HEAD_DIM), lambda i: (0, 0)),
            pl.BlockSpec((1, HEAD_DIM), lambda i: (0, 0)),
            pl.BlockSpec((tm, HEAD_DIM), pos_block),
            pl.BlockSpec((tm, HEAD_DIM), pos_block),
        ],
        out_specs=[
            pl.BlockSpec((tm, n_heads * HEAD_DIM), lambda i: (i, 0)),
            pl.BlockSpec((tm, n_kv * HEAD_DIM), lambda i: (i, 0)),
            pl.BlockSpec((tm, n_kv * HEAD_DIM), lambda i: (i, 0)),
        ],
        out_shape=[
            jax.ShapeDtypeStruct((t, n_heads * HEAD_DIM), BF16),
            jax.ShapeDtypeStruct((t, n_kv * HEAD_DIM), BF16),
            jax.ShapeDtypeStruct((t, n_kv * HEAD_DIM), BF16),
        ],
        compiler_params=_params(("arbitrary",)),
        name="qkv_norm_rope",
    )(x, gain.reshape(1, d), mod, mod, w_qkv, q_g.reshape(1, HEAD_DIM), k_g.reshape(1, HEAD_DIM), cos, sin)


def _flash_kernel(q_ref, k_ref, v_ref, o_ref, m_ref, l_ref, acc_ref, *, tq, tk, scale):
    s_len = k_ref.shape[0]
    q = jnp.concatenate([q_ref[:, g * HEAD_DIM:(g + 1) * HEAD_DIM] for g in range(GQA_GROUP)], axis=0)
    m_ref[...] = jnp.full_like(m_ref, -jnp.inf)
    l_ref[...] = jnp.zeros_like(l_ref)
    acc_ref[...] = jnp.zeros_like(acc_ref)

    def body(j, carry):
        r0 = pl.multiple_of(j * tk, tk)
        ks = k_ref[pl.ds(r0, tk), :]
        vs = v_ref[pl.ds(r0, tk), :]
        s = lax.dot_general(q, ks, (((1,), (1,)), ((), ())), preferred_element_type=F32) * scale
        m_prev = m_ref[...]
        m_new = jnp.maximum(m_prev, jnp.max(s, axis=-1, keepdims=True))
        alpha = jnp.exp(m_prev - m_new)
        p = jnp.exp(s - m_new)
        l_ref[...] = alpha * l_ref[...] + jnp.sum(p, axis=-1, keepdims=True)
        acc_ref[...] = alpha * acc_ref[...] + jnp.dot(p.astype(BF16), vs, preferred_element_type=F32)
        m_ref[...] = m_new
        return carry

    lax.fori_loop(0, s_len // tk, body, 0)
    o = acc_ref[...] / l_ref[...]
    for g in range(GQA_GROUP):
        o_ref[:, g * HEAD_DIM:(g + 1) * HEAD_DIM] = o[g * tq:(g + 1) * tq, :].astype(o_ref.dtype)


def _flash_group(q, k, v, o_prev, row_off, batch, s_len, tq=256, tk=512):
    t, dq = q.shape
    n_kv = k.shape[1] // HEAD_DIM
    qw = GQA_GROUP * HEAD_DIM
    nq = s_len // tq
    qoff = row_off // tq
    soff = row_off // s_len
    kern = functools.partial(_flash_kernel, tq=tq, tk=tk, scale=1.0 / math.sqrt(HEAD_DIM))
    in_specs = [
        pl.BlockSpec((tq, qw), lambda b, h, i: (qoff + b * nq + i, h)),
        pl.BlockSpec((s_len, HEAD_DIM), lambda b, h, i: (soff + b, h)),
        pl.BlockSpec((s_len, HEAD_DIM), lambda b, h, i: (soff + b, h)),
    ]
    args = [q, k, v]
    aliases = {}
    if o_prev is not None:
        in_specs.append(pl.BlockSpec(memory_space=pl.ANY))
        args.append(o_prev)
        aliases = {3: 0}
        kern = functools.partial(_drop_arg_kernel, kern)
    return pl.pallas_call(
        kern,
        grid=(batch, n_kv, nq),
        in_specs=in_specs,
        out_specs=pl.BlockSpec((tq, qw), lambda b, h, i: (qoff + b * nq + i, h)),
        out_shape=jax.ShapeDtypeStruct((t, dq), BF16),
        scratch_shapes=[
            pltpu.VMEM((GQA_GROUP * tq, 1), F32),
            pltpu.VMEM((GQA_GROUP * tq, 1), F32),
            pltpu.VMEM((GQA_GROUP * tq, HEAD_DIM), F32),
        ],
        input_output_aliases=aliases,
        compiler_params=_params(("arbitrary", "arbitrary", "arbitrary")),
        name="flash_attention",
    )(*args)


def _drop_arg_kernel(kern, q_ref, k_ref, v_ref, prev_ref, o_ref, *scratch):
    del prev_ref
    kern(q_ref, k_ref, v_ref, o_ref, *scratch)


def _oproj_kernel(x_ref, a_ref, gt_ref, w_ref, o_ref):
    m = jnp.dot(a_ref[...], w_ref[...], preferred_element_type=F32)
    o_ref[...] = x_ref[...] + gt_ref[0] * m


def _oproj(x, attn, mod, sub, w_o, seg_rows, tm=512):
    t, d = x.shape
    return pl.pallas_call(
        _oproj_kernel,
        grid=(t // tm,),
        in_specs=[
            pl.BlockSpec((tm, d), lambda i: (i, 0)),
            pl.BlockSpec((tm, attn.shape[1]), lambda i: (i, 0)),
            _mod_specs(tm, seg_rows, sub, d, 1)[2],
            pl.BlockSpec(w_o.shape, lambda i: (0, 0)),
        ],
        out_specs=pl.BlockSpec((tm, d), lambda i: (i, 0)),
        out_shape=jax.ShapeDtypeStruct((t, d), F32),
        compiler_params=_params(("arbitrary",)),
        name="attn_out_proj",
    )(x, attn, mod, w_o)


def kernel(x_prompt, x_sample, c_prompt, c_sample, ada_w, ada_b, norm_g, ffn_w_in, ffn_w_out,
           pool_w, pool_scale, attn_w_qkv, attn_q_g, attn_k_g, attn_w_o):
    bp, sp, d = x_prompt.shape
    bs, ss, _ = x_sample.shape
    depth = ada_w.shape[0]
    group_rows = bp * sp
    seq_lens = (sp, ss)
    seg_rows = math.gcd(sp, ss)
    assert group_rows % seg_rows == 0 and sp % GRID_W == 0 and ss % GRID_W == 0

    x = jnp.concatenate([x_prompt.reshape(bp * sp, d), x_sample.reshape(bs * ss, d)], axis=0)
    seg_seq = [b for b in range(bp) for _ in range(sp // seg_rows)] + \
              [bp + b for b in range(bs) for _ in range(ss // seg_rows)]
    c_seg = jnp.concatenate([c_prompt, c_sample], axis=0)[jnp.array(seg_seq)]
    nseg = len(seg_seq)
    mod_all = _ada_mod(c_seg, ada_w, ada_b)
    cos, sin = _rope_tables(max(sp, ss))

    w_in = ffn_w_in.astype(BF16)
    w_out = ffn_w_out.astype(BF16)
    pool_wb = pool_w.astype(BF16)
    w_qkv = attn_w_qkv.astype(BF16)
    w_o = attn_w_o.astype(BF16)

    for i in range(depth):
        mod = mod_all[i].reshape(nseg * N_MOD_ROWS, 1, d)
        j = i // 2
        x = _ffn(x, norm_g[i, 0], mod, 0, w_in, w_out, i, 0, seg_rows)
        if i % 2 == 0:
            x = _pool(x, norm_g[i, 1], mod, 1, pool_wb[j], pool_scale[j], seg_rows, seq_lens, group_rows)
        else:
            q, k, v = _qkv(x, norm_g[i, 1], mod, 1, w_qkv[j], attn_q_g[j], attn_k_g[j], cos, sin,
                           seg_rows, seq_lens, group_rows)
            a = _flash_group(q, k, v, None, 0, bp, sp)
            a = _flash_group(q, k, v, a, group_rows, bs, ss)
            x = _oproj(x, a, mod, 1, w_o[j], seg_rows)
        x = _ffn(x, norm_g[i, 2], mod, 2, w_in, w_out, i, 1, seg_rows)

    y_prompt = x[:group_rows].reshape(bp, sp, d)
    y_sample = x[group_rows:].reshape(bs, ss, d)
    return (y_prompt, y_sample)
```

```python
import functools
import math

import jax
import jax.numpy as jnp
from jax import lax
from jax.experimental import pallas as pl
from jax.experimental.pallas import tpu as pltpu

F32 = jnp.float32
BF16 = jnp.bfloat16

EPS = 1e-6
HEAD_DIM = 128
AXIS_DIM = HEAD_DIM // 2
ROPE_THETA = 10000.0
GRID_W = 64
GQA_GROUP = 4
POOL_WINDOWS = (2, 4, 8, 16)
POOL_HALO = 8
N_MOD_ROWS = 9

VMEM_LIMIT_BYTES = 60 * 1024 * 1024


def _params(semantics):
    return pltpu.CompilerParams(dimension_semantics=semantics, vmem_limit_bytes=VMEM_LIMIT_BYTES)


def _mod_norm(x, g, shift, scl):
    ms = jnp.mean(x * x, axis=-1, keepdims=True)
    y = x * lax.rsqrt(ms + EPS)
    return (y * g) * (1.0 + scl) + shift


def _ada_kernel(c_ref, w_ref, b_ref, o_ref):
    c = c_ref[...]
    s = (c * jax.nn.sigmoid(c)).astype(BF16)
    w = w_ref[0].astype(BF16)
    o_ref[0] = jnp.dot(s, w, preferred_element_type=F32) + b_ref[0]


def _ada_mod(c_seg, ada_w, ada_b, tn=1024):
    depth, d, n = ada_w.shape
    nseg = c_seg.shape[0]
    return pl.pallas_call(
        _ada_kernel,
        grid=(depth, n // tn),
        in_specs=[
            pl.BlockSpec((nseg, d), lambda l, j: (0, 0)),
            pl.BlockSpec((1, d, tn), lambda l, j: (l, 0, j)),
            pl.BlockSpec((1, 1, tn), lambda l, j: (l, 0, j)),
        ],
        out_specs=pl.BlockSpec((1, nseg, tn), lambda l, j: (l, 0, j)),
        out_shape=jax.ShapeDtypeStruct((depth, nseg, n), F32),
        compiler_params=_params(("arbitrary", "arbitrary")),
        name="ada_mod",
    )(c_seg, ada_w, ada_b.reshape(depth, 1, n))


def _mod_specs(tm, seg_rows, sub, d, ngrid):
    specs = []
    for r in range(3):
        row = sub * 3 + r
        if ngrid == 1:
            imap = lambda i, row=row: ((i * tm) // seg_rows * N_MOD_ROWS + row, 0, 0)
        else:
            imap = lambda i, k, row=row: ((i * tm) // seg_rows * N_MOD_ROWS + row, 0, 0)
        specs.append(pl.BlockSpec((1, 1, d), imap))
    return specs


def _ffn_kernel(x_ref, g_ref, sh_ref, sc_ref, gt_ref, wg_ref, wu_ref, wo_ref, o_ref, h_ref):
    k = pl.program_id(1)

    @pl.when(k == 0)
    def _():
        h = _mod_norm(x_ref[...], g_ref[...], sh_ref[0], sc_ref[0])
        h_ref[...] = h.astype(BF16)
        o_ref[...] = jnp.zeros_like(o_ref)

    hb = h_ref[...]
    g = jnp.dot(hb, wg_ref[...], preferred_element_type=F32)
    u = jnp.dot(hb, wu_ref[...], preferred_element_type=F32)
    a = (g * jax.nn.sigmoid(g) * u).astype(BF16)
    tn = wg_ref.shape[1]
    for c in range(o_ref.shape[1] // tn):
        cols = slice(c * tn, (c + 1) * tn)
        o_ref[:, cols] += jnp.dot(a, wo_ref[:, cols], preferred_element_type=F32)

    @pl.when(k == pl.num_programs(1) - 1)
    def _():
        o_ref[...] = x_ref[...] + (0.5 * gt_ref[0]) * o_ref[...]


def _ffn(x, gain, mod, sub, w_in, w_out, li, fi, seg_rows, tm=512, tf=512):
    t, d = x.shape
    d_ff = w_out.shape[2]
    nk = d_ff // tf
    return pl.pallas_call(
        _ffn_kernel,
        grid=(t // tm, nk),
        in_specs=[
            pl.BlockSpec((tm, d), lambda i, k: (i, 0)),
            pl.BlockSpec((1, d), lambda i, k: (0, 0)),
            *_mod_specs(tm, seg_rows, sub, d, 2),
            pl.BlockSpec((None, None, d, tf), lambda i, k: (li, fi, 0, k)),
            pl.BlockSpec((None, None, d, tf), lambda i, k: (li, fi, 0, k + nk)),
            pl.BlockSpec((None, None, tf, d), lambda i, k: (li, fi, k, 0)),
        ],
        out_specs=pl.BlockSpec((tm, d), lambda i, k: (i, 0)),
        out_shape=jax.ShapeDtypeStruct((t, d), F32),
        scratch_shapes=[pltpu.VMEM((tm, d), BF16)],
        compiler_params=_params(("arbitrary", "arbitrary")),
        name="ffn",
    )(x, gain.reshape(1, d), mod, mod, mod, w_in, w_in, w_out)


def _pool_kernel(x_ref, xp_ref, xn_ref, g_ref, sh_ref, sc_ref, gt_ref, w_ref, ps_ref, o_ref, hbuf,
                 *, tp, seq_lens, group_rows):
    i = pl.program_id(0)
    row0 = i * tp
    in_first = row0 < group_rows
    slen = jnp.where(in_first, seq_lens[0], seq_lens[1])
    pos0 = jnp.where(in_first, row0 % seq_lens[0], (row0 - group_rows) % seq_lens[1])
    g = g_ref[...]
    sh = sh_ref[0]
    sc = sc_ref[0]
    x = x_ref[...]
    d = x.shape[-1]
    gc = d // len(POOL_WINDOWS)

    has_prev = pos0 > 0
    has_next = pos0 + tp < slen
    hp = _mod_norm(xp_ref[...], g, sh, sc)
    hn = _mod_norm(xn_ref[...], g, sh, sc)
    hbuf[0:POOL_HALO, :] = jnp.where(has_prev, hp, 0.0)
    hbuf[POOL_HALO:POOL_HALO + tp, :] = _mod_norm(x, g, sh, sc)
    hbuf[POOL_HALO + tp:, :] = jnp.where(has_next, hn, 0.0)

    pos = pos0 + lax.broadcasted_iota(jnp.int32, (tp, 1), 0)
    outs = []
    for gi, w in enumerate(POOL_WINDOWS):
        cols = slice(gi * gc, (gi + 1) * gc)
        acc = hbuf[POOL_HALO - w // 2:POOL_HALO - w // 2 + tp, cols]
        for j in range(-w // 2 + 1, w // 2):
            acc = acc + hbuf[POOL_HALO + j:POOL_HALO + j + tp, cols]
        lo = jnp.maximum(pos - w // 2, 0)
        hi = jnp.minimum(pos + w // 2, slen)
        cnt = (hi - lo).astype(F32)
        p = acc / cnt - hbuf[POOL_HALO:POOL_HALO + tp, cols]
        outs.append(jnp.dot(p.astype(BF16), w_ref[gi], preferred_element_type=F32))
    m = jnp.concatenate(outs, axis=-1) * ps_ref[...]
    o_ref[...] = x + gt_ref[0] * m


def _pool(x, gain, mod, sub, pool_w, pool_scale, seg_rows, seq_lens, group_rows, tp=256):
    t, d = x.shape
    hb = tp // POOL_HALO
    nhalo = t // POOL_HALO
    kern = functools.partial(_pool_kernel, tp=tp, seq_lens=seq_lens, group_rows=group_rows)
    return pl.pallas_call(
        kern,
        grid=(t // tp,),
        in_specs=[
            pl.BlockSpec((tp, d), lambda i: (i, 0)),
            pl.BlockSpec((POOL_HALO, d), lambda i: (jnp.maximum(i * hb - 1, 0), 0)),
            pl.BlockSpec((POOL_HALO, d), lambda i: (jnp.minimum((i + 1) * hb, nhalo - 1), 0)),
            pl.BlockSpec((1, d), lambda i: (0, 0)),
            *_mod_specs(tp, seg_rows, sub, d, 1),
            pl.BlockSpec(pool_w.shape, lambda i: (0, 0, 0)),
            pl.BlockSpec((1, d), lambda i: (0, 0)),
        ],
        out_specs=pl.BlockSpec((tp, d), lambda i: (i, 0)),
        out_shape=jax.ShapeDtypeStruct((t, d), F32),
        scratch_shapes=[pltpu.VMEM((tp + 2 * POOL_HALO, d), F32)],
        compiler_params=_params(("arbitrary",)),
        name="pool_mixer",
    )(x, x, x, gain.reshape(1, d), mod, mod, mod, pool_w, pool_scale.reshape(1, d))


def _rope_tables(s):
    t = jnp.arange(s)
    r = (t // GRID_W).astype(F32)
    c = (t % GRID_W).astype(F32)
    inv = ROPE_THETA ** (-jnp.arange(0, AXIS_DIM, 2, dtype=F32) / AXIS_DIM)
    ang_r = r[:, None] * inv[None, :]
    ang_c = c[:, None] * inv[None, :]
    cos = jnp.concatenate([jnp.cos(ang_r)] * 2 + [jnp.cos(ang_c)] * 2, axis=-1)
    sin = jnp.concatenate([-jnp.sin(ang_r), jnp.sin(ang_r), -jnp.sin(ang_c), jnp.sin(ang_c)], axis=-1)
    return cos, sin


def _head_norm_rope(xh, gain, cos, sin, first_half):
    ms = jnp.mean(xh * xh, axis=-1, keepdims=True)
    y = xh * lax.rsqrt(ms + EPS) * gain
    half = AXIS_DIM // 2
    partner = jnp.where(first_half, pltpu.roll(y, HEAD_DIM - half, 1), pltpu.roll(y, half, 1))
    return y * cos + partner * sin


def _qkv_kernel(x_ref, g_ref, sh_ref, sc_ref, w_ref, qg_ref, kg_ref, cos_ref, sin_ref,
                q_ref, k_ref, v_ref, *, n_heads, n_kv):
    h = _mod_norm(x_ref[...], g_ref[...], sh_ref[0], sc_ref[0]).astype(BF16)
    qkv = jnp.dot(h, w_ref[...], preferred_element_type=F32)
    cos = cos_ref[...]
    sin = sin_ref[...]
    lane = lax.broadcasted_iota(jnp.int32, cos.shape, 1)
    first_half = (lane % AXIS_DIM) < (AXIS_DIM // 2)
    qg = qg_ref[...]
    kg = kg_ref[...]
    for hd in range(n_heads):
        cols = slice(hd * HEAD_DIM, (hd + 1) * HEAD_DIM)
        q_ref[:, cols] = _head_norm_rope(qkv[:, cols], qg, cos, sin, first_half).astype(BF16)
    k0 = n_heads * HEAD_DIM
    for hd in range(n_kv):
        cols = slice(k0 + hd * HEAD_DIM, k0 + (hd + 1) * HEAD_DIM)
        k_ref[:, hd * HEAD_DIM:(hd + 1) * HEAD_DIM] = _head_norm_rope(
            qkv[:, cols], kg, cos, sin, first_half).astype(BF16)
    v_ref[...] = qkv[:, k0 + n_kv * HEAD_DIM:].astype(BF16)


def _qkv(x, gain, mod, sub, w_qkv, q_g, k_g, cos, sin, seg_rows, seq_lens, group_rows, tm=512):
    t, d = x.shape
    n_tot = w_qkv.shape[1] // HEAD_DIM
    n_heads = d // HEAD_DIM
    n_kv = (n_tot - n_heads) // 2

    def pos_block(i):
        row0 = i * tm
        pos0 = jnp.where(row0 < group_rows, row0 % seq_lens[0], (row0 - group_rows) % seq_lens[1])
        return (pos0 // tm, 0)

    kern = functools.partial(_qkv_kernel, n_heads=n_heads, n_kv=n_kv)
    return pl.pallas_call(
        kern,
        grid=(t // tm,),
        in_specs=[
            pl.BlockSpec((tm, d), lambda i: (i, 0)),
            pl.BlockSpec((1, d), lambda i: (0, 0)),
            *_mod_specs(tm, seg_rows, sub, d, 1)[:2],
            pl.BlockSpec(w_qkv.shape, lambda i: (0, 0)),
            pl.BlockSpec((1, HEAD_DIM), lambda i: (0, 0)),
            pl.BlockSpec((1, HEAD_DIM), lambda i: (0, 0)),
            pl.BlockSpec((tm, HEAD_DIM), pos_block),
            pl.BlockSpec((tm, HEAD_DIM), pos_block),
        ],
        out_specs=[
            pl.BlockSpec((tm, n_heads * HEAD_DIM), lambda i: (i, 0)),
            pl.BlockSpec((tm, n_kv * HEAD_DIM), lambda i: (i, 0)),
            pl.BlockSpec((tm, n_kv * HEAD_DIM), lambda i: (i, 0)),
        ],
        out_shape=[
            jax.ShapeDtypeStruct((t, n_heads * HEAD_DIM), BF16),
            jax.ShapeDtypeStruct((t, n_kv * HEAD_DIM), BF16),
            jax.ShapeDtypeStruct((t, n_kv * HEAD_DIM), BF16),
        ],
        compiler_params=_params(("arbitrary",)),
        name="qkv_norm_rope",
    )(x, gain.reshape(1, d), mod, mod, w_qkv, q_g.reshape(1, HEAD_DIM), k_g.reshape(1, HEAD_DIM), cos, sin)


def _flash_kernel(q_ref, k_ref, v_ref, o_ref, *, tk, scale):
    s_len = k_ref.shape[0]
    nblk = tk // HEAD_DIM
    chunks = []
    for r0 in range(0, s_len, tk):
        vs = v_ref[r0:r0 + tk, :]
        chunks.append((k_ref[r0:r0 + tk, :], jnp.concatenate([vs, jnp.ones_like(vs)], axis=-1)))
    for g in range(GQA_GROUP):
        cols = slice(g * HEAD_DIM, (g + 1) * HEAD_DIM)
        q = q_ref[:, cols]
        m = acc = None
        for ks, v_ext in chunks:
            s = lax.dot_general(q, ks, (((1,), (1,)), ((), ())), preferred_element_type=F32) * scale
            blocks = [s[:, c * HEAD_DIM:(c + 1) * HEAD_DIM] for c in range(nblk)]
            m_cur = jnp.max(functools.reduce(jnp.maximum, blocks), axis=-1, keepdims=True)
            m_new = jnp.broadcast_to(m_cur, blocks[0].shape) if m is None else jnp.maximum(m, m_cur)
            p = jnp.concatenate([jnp.exp(blk - m_new) for blk in blocks], axis=-1).astype(BF16)
            pv = jnp.dot(p, v_ext, preferred_element_type=F32)
            if m is None:
                acc = pv
            else:
                alpha = jnp.exp(m - m_new)
                acc = jnp.concatenate([alpha, alpha], axis=-1) * acc + pv
            m = m_new
        o_ref[:, cols] = (acc[:, :HEAD_DIM] / acc[:, HEAD_DIM:]).astype(o_ref.dtype)


def _flash_group(q, k, v, o_prev, row_off, batch, s_len, tq=512, tk=2048):
    t, dq = q.shape
    n_kv = k.shape[1] // HEAD_DIM
    qw = GQA_GROUP * HEAD_DIM
    nq = s_len // tq
    qoff = row_off // tq
    soff = row_off // s_len
    kern = functools.partial(_flash_kernel, tk=tk, scale=1.0 / math.sqrt(HEAD_DIM))
    in_specs = [
        pl.BlockSpec((tq, qw), lambda b, h, i: (qoff + b * nq + i, h)),
        pl.BlockSpec((s_len, HEAD_DIM), lambda b, h, i: (soff + b, h)),
        pl.BlockSpec((s_len, HEAD_DIM), lambda b, h, i: (soff + b, h)),
    ]
    args = [q, k, v]
    aliases = {}
    if o_prev is not None:
        in_specs.append(pl.BlockSpec(memory_space=pl.ANY))
        args.append(o_prev)
        aliases = {3: 0}
        kern = functools.partial(_drop_arg_kernel, kern)
    return pl.pallas_call(
        kern,
        grid=(batch, n_kv, nq),
        in_specs=in_specs,
        out_specs=pl.BlockSpec((tq, qw), lambda b, h, i: (qoff + b * nq + i, h)),
        out_shape=jax.ShapeDtypeStruct((t, dq), BF16),
        input_output_aliases=aliases,
        compiler_params=_params(("arbitrary", "arbitrary", "arbitrary")),
        name="flash_attention",
    )(*args)


def _drop_arg_kernel(kern, q_ref, k_ref, v_ref, prev_ref, o_ref):
    del prev_ref
    kern(q_ref, k_ref, v_ref, o_ref)


def _oproj_kernel(x_ref, a_ref, gt_ref, w_ref, o_ref):
    m = jnp.dot(a_ref[...], w_ref[...], preferred_element_type=F32)
    o_ref[...] = x_ref[...] + gt_ref[0] * m


def _oproj(x, attn, mod, sub, w_o, seg_rows, tm=512):
    t, d = x.shape
    return pl.pallas_call(
        _oproj_kernel,
        grid=(t // tm,),
        in_specs=[
            pl.BlockSpec((tm, d), lambda i: (i, 0)),
            pl.BlockSpec((tm, attn.shape[1]), lambda i: (i, 0)),
            _mod_specs(tm, seg_rows, sub, d, 1)[2],
            pl.BlockSpec(w_o.shape, lambda i: (0, 0)),
        ],
        out_specs=pl.BlockSpec((tm, d), lambda i: (i, 0)),
        out_shape=jax.ShapeDtypeStruct((t, d), F32),
        compiler_params=_params(("arbitrary",)),
        name="attn_out_proj",
    )(x, attn, mod, w_o)


def kernel(x_prompt, x_sample, c_prompt, c_sample, ada_w, ada_b, norm_g, ffn_w_in, ffn_w_out,
           pool_w, pool_scale, attn_w_qkv, attn_q_g, attn_k_g, attn_w_o):
    bp, sp, d = x_prompt.shape
    bs, ss, _ = x_sample.shape
    depth = ada_w.shape[0]
    group_rows = bp * sp
    seq_lens = (sp, ss)
    seg_rows = math.gcd(sp, ss)
    assert group_rows % seg_rows == 0 and sp % GRID_W == 0 and ss % GRID_W == 0

    x = jnp.concatenate([x_prompt.reshape(bp * sp, d), x_sample.reshape(bs * ss, d)], axis=0)
    seg_seq = [b for b in range(bp) for _ in range(sp // seg_rows)] + \
              [bp + b for b in range(bs) for _ in range(ss // seg_rows)]
    c_seg = jnp.concatenate([c_prompt, c_sample], axis=0)[jnp.array(seg_seq)]
    nseg = len(seg_seq)
    mod_all = _ada_mod(c_seg, ada_w, ada_b)
    cos, sin = _rope_tables(max(sp, ss))

    w_in = ffn_w_in.astype(BF16)
    w_out = ffn_w_out.astype(BF16)
    pool_wb = pool_w.astype(BF16)
    w_qkv = attn_w_qkv.astype(BF16)
    w_o = attn_w_o.astype(BF16)

    for i in range(depth):
        mod = mod_all[i].reshape(nseg * N_MOD_ROWS, 1, d)
        j = i // 2
        x = _ffn(x, norm_g[i, 0], mod, 0, w_in, w_out, i, 0, seg_rows)
        if i % 2 == 0:
            x = _pool(x, norm_g[i, 1], mod, 1, pool_wb[j], pool_scale[j], seg_rows, seq_lens, group_rows)
        else:
            q, k, v = _qkv(x, norm_g[i, 1], mod, 1, w_qkv[j], attn_q_g[j], attn_k_g[j], cos, sin,
                           seg_rows, seq_lens, group_rows)
            a = _flash_group(q, k, v, None, 0, bp, sp)
            a = _flash_group(q, k, v, a, group_rows, bs, ss)
            x = _oproj(x, a, mod, 1, w_o[j], seg_rows)
        x = _ffn(x, norm_g[i, 2], mod, 2, w_in, w_out, i, 1, seg_rows)

    y_prompt = x[:group_rows].reshape(bp, sp, d)
    y_sample = x[group_rows:].reshape(bs, ss, d)
    return (y_prompt, y_sample)
```

```python
import functools
import math

import jax
import jax.numpy as jnp
from jax import lax
from jax.experimental import pallas as pl
from jax.experimental.pallas import tpu as pltpu

F32 = jnp.float32
BF16 = jnp.bfloat16

EPS = 1e-6
HEAD_DIM = 128
AXIS_DIM = HEAD_DIM // 2
ROPE_THETA = 10000.0
GRID_W = 64
GQA_GROUP = 4
POOL_WINDOWS = (2, 4, 8, 16)
POOL_HALO = 8
Q_SCALE = math.log2(math.e) / math.sqrt(HEAD_DIM)
N_MOD_ROWS = 9
NORM_ROWS = 256

VMEM_LIMIT_BYTES = 60 * 1024 * 1024


def _params(semantics):
    return pltpu.CompilerParams(dimension_semantics=semantics, vmem_limit_bytes=VMEM_LIMIT_BYTES)


def _mod_norm(x, g, shift, scl):
    ms = jnp.mean(x * x, axis=-1, keepdims=True)
    y = x * lax.rsqrt(ms + EPS)
    return (y * g) * (1.0 + scl) + shift


def _ada_kernel(c_ref, w_ref, b_ref, o_ref):
    c = c_ref[...]
    s = (c * jax.nn.sigmoid(c)).astype(BF16)
    w = w_ref[0].astype(BF16)
    o_ref[0] = jnp.dot(s, w, preferred_element_type=F32) + b_ref[0]


def _ada_mod(c_seg, ada_w, ada_b, tn=1024):
    depth, d, n = ada_w.shape
    nseg = c_seg.shape[0]
    return pl.pallas_call(
        _ada_kernel,
        grid=(depth, n // tn),
        in_specs=[
            pl.BlockSpec((nseg, d), lambda l, j: (0, 0)),
            pl.BlockSpec((1, d, tn), lambda l, j: (l, 0, j)),
            pl.BlockSpec((1, 1, tn), lambda l, j: (l, 0, j)),
        ],
        out_specs=pl.BlockSpec((1, nseg, tn), lambda l, j: (l, 0, j)),
        out_shape=jax.ShapeDtypeStruct((depth, nseg, n), F32),
        compiler_params=_params(("arbitrary", "arbitrary")),
        name="ada_mod",
    )(c_seg, ada_w, ada_b.reshape(depth, 1, n))


def _mod_specs(tm, seg_rows, sub, d, ngrid):
    specs = []
    for r in range(3):
        row = sub * 3 + r
        if ngrid == 1:
            imap = lambda i, row=row: ((i * tm) // seg_rows * N_MOD_ROWS + row, 0, 0)
        else:
            imap = lambda i, k, row=row: ((i * tm) // seg_rows * N_MOD_ROWS + row, 0, 0)
        specs.append(pl.BlockSpec((1, 1, d), imap))
    return specs


def _ffn_kernel(x_ref, g_ref, sh_ref, sc_ref, gt_ref, wg_ref, wu_ref, wo_ref, o_ref, h_ref, inv_ref,
                *, n_split):
    k = pl.program_id(1)

    @pl.when(k == 0)
    def _():
        n_chunks = h_ref.shape[0] // NORM_ROWS
        gmod = g_ref[...] * (1.0 + sc_ref[0])
        shift = sh_ref[0]

        def stats(r, carry):
            rows = pl.ds(pl.multiple_of(r * NORM_ROWS, NORM_ROWS), NORM_ROWS)
            x = x_ref[rows, :]
            inv_ref[rows, :] = lax.rsqrt(jnp.mean(x * x, axis=-1, keepdims=True) + EPS)
            return carry

        def apply(r, carry):
            rows = pl.ds(pl.multiple_of(r * NORM_ROWS, NORM_ROWS), NORM_ROWS)
            h_ref[rows, :] = ((x_ref[rows, :] * inv_ref[rows, :]) * gmod + shift).astype(BF16)
            o_ref[rows, :] = jnp.zeros((NORM_ROWS, o_ref.shape[1]), F32)
            return carry

        lax.fori_loop(0, n_chunks, stats, 0)
        lax.fori_loop(0, n_chunks, apply, 0)

    rs = h_ref.shape[0] // n_split
    tn = wg_ref.shape[1]
    for r in range(n_split):
        rows = slice(r * rs, (r + 1) * rs)
        hb = h_ref[rows, :]
        g = jnp.dot(hb, wg_ref[...], preferred_element_type=F32)
        u = jnp.dot(hb, wu_ref[...], preferred_element_type=F32)
        a = (g * jax.nn.sigmoid(g) * u).astype(BF16)
        for c in range(o_ref.shape[1] // tn):
            cols = slice(c * tn, (c + 1) * tn)
            o_ref[rows, cols] += jnp.dot(a, wo_ref[:, cols], preferred_element_type=F32)

    @pl.when(k == pl.num_programs(1) - 1)
    def _():
        o_ref[...] = x_ref[...] + (0.5 * gt_ref[0]) * o_ref[...]


def _ffn(x, gain, mod, sub, w_in, w_out, li, fi, seg_rows, tm=1024, tf=512, n_split=2):
    t, d = x.shape
    d_ff = w_out.shape[2]
    nk = d_ff // tf
    return pl.pallas_call(
        functools.partial(_ffn_kernel, n_split=n_split),
        grid=(t // tm, nk),
        in_specs=[
            pl.BlockSpec((tm, d), lambda i, k: (i, 0)),
            pl.BlockSpec((1, d), lambda i, k: (0, 0)),
            *_mod_specs(tm, seg_rows, sub, d, 2),
            pl.BlockSpec((None, None, d, tf), lambda i, k: (li, fi, 0, k)),
            pl.BlockSpec((None, None, d, tf), lambda i, k: (li, fi, 0, k + nk)),
            pl.BlockSpec((None, None, tf, d), lambda i, k: (li, fi, k, 0)),
        ],
        out_specs=pl.BlockSpec((tm, d), lambda i, k: (i, 0)),
        out_shape=jax.ShapeDtypeStruct((t, d), F32),
        scratch_shapes=[pltpu.VMEM((tm, d), BF16), pltpu.VMEM((tm, 1), F32)],
        compiler_params=_params(("arbitrary", "arbitrary")),
        name="ffn",
    )(x, gain.reshape(1, d), mod, mod, mod, w_in, w_in, w_out)


def _pool_kernel(x_ref, xp_ref, xn_ref, g_ref, sh_ref, sc_ref, gt_ref, w_ref, ps_ref, o_ref, hbuf,
                 *, tp, seq_lens, group_rows):
    i = pl.program_id(0)
    row0 = i * tp
    in_first = row0 < group_rows
    slen = jnp.where(in_first, seq_lens[0], seq_lens[1])
    pos0 = jnp.where(in_first, row0 % seq_lens[0], (row0 - group_rows) % seq_lens[1])
    g = g_ref[...]
    sh = sh_ref[0]
    sc = sc_ref[0]
    x = x_ref[...]
    d = x.shape[-1]
    gc = d // len(POOL_WINDOWS)

    has_prev = pos0 > 0
    has_next = pos0 + tp < slen
    hp = _mod_norm(xp_ref[...], g, sh, sc)
    hn = _mod_norm(xn_ref[...], g, sh, sc)
    hbuf[0:POOL_HALO, :] = jnp.where(has_prev, hp, 0.0)
    hbuf[POOL_HALO:POOL_HALO + tp, :] = _mod_norm(x, g, sh, sc)
    hbuf[POOL_HALO + tp:, :] = jnp.where(has_next, hn, 0.0)

    pos = pos0 + lax.broadcasted_iota(jnp.int32, (tp, 1), 0)
    outs = []
    for gi, w in enumerate(POOL_WINDOWS):
        cols = slice(gi * gc, (gi + 1) * gc)
        acc = hbuf[POOL_HALO - w // 2:POOL_HALO - w // 2 + tp, cols]
        for j in range(-w // 2 + 1, w // 2):
            acc = acc + hbuf[POOL_HALO + j:POOL_HALO + j + tp, cols]
        lo = jnp.maximum(pos - w // 2, 0)
        hi = jnp.minimum(pos + w // 2, slen)
        cnt = (hi - lo).astype(F32)
        p = acc / cnt - hbuf[POOL_HALO:POOL_HALO + tp, cols]
        outs.append(jnp.dot(p.astype(BF16), w_ref[gi], preferred_element_type=F32))
    m = jnp.concatenate(outs, axis=-1) * ps_ref[...]
    o_ref[...] = x + gt_ref[0] * m


def _pool(x, gain, mod, sub, pool_w, pool_scale, seg_rows, seq_lens, group_rows, tp=256):
    t, d = x.shape
    hb = tp // POOL_HALO
    nhalo = t // POOL_HALO
    kern = functools.partial(_pool_kernel, tp=tp, seq_lens=seq_lens, group_rows=group_rows)
    return pl.pallas_call(
        kern,
        grid=(t // tp,),
        in_specs=[
            pl.BlockSpec((tp, d), lambda i: (i, 0)),
            pl.BlockSpec((POOL_HALO, d), lambda i: (jnp.maximum(i * hb - 1, 0), 0)),
            pl.BlockSpec((POOL_HALO, d), lambda i: (jnp.minimum((i + 1) * hb, nhalo - 1), 0)),
            pl.BlockSpec((1, d), lambda i: (0, 0)),
            *_mod_specs(tp, seg_rows, sub, d, 1),
            pl.BlockSpec(pool_w.shape, lambda i: (0, 0, 0)),
            pl.BlockSpec((1, d), lambda i: (0, 0)),
        ],
        out_specs=pl.BlockSpec((tp, d), lambda i: (i, 0)),
        out_shape=jax.ShapeDtypeStruct((t, d), F32),
        scratch_shapes=[pltpu.VMEM((tp + 2 * POOL_HALO, d), F32)],
        compiler_params=_params(("arbitrary",)),
        name="pool_mixer",
    )(x, x, x, gain.reshape(1, d), mod, mod, mod, pool_w, pool_scale.reshape(1, d))


def _rope_tables(s):
    t = jnp.arange(s)
    r = (t // GRID_W).astype(F32)
    c = (t % GRID_W).astype(F32)
    inv = ROPE_THETA ** (-jnp.arange(0, AXIS_DIM, 2, dtype=F32) / AXIS_DIM)
    ang_r = r[:, None] * inv[None, :]
    ang_c = c[:, None] * inv[None, :]
    cos = jnp.concatenate([jnp.cos(ang_r)] * 2 + [jnp.cos(ang_c)] * 2, axis=-1)
    sin = jnp.concatenate([-jnp.sin(ang_r), jnp.sin(ang_r), -jnp.sin(ang_c), jnp.sin(ang_c)], axis=-1)
    return cos, sin


def _head_norm_rope(xh, gain, cos, sin, first_half):
    ms = jnp.mean(xh * xh, axis=-1, keepdims=True)
    y = xh * lax.rsqrt(ms + EPS) * gain
    half = AXIS_DIM // 2
    partner = jnp.where(first_half, pltpu.roll(y, HEAD_DIM - half, 1), pltpu.roll(y, half, 1))
    return y * cos + partner * sin


def _qkv_kernel(x_ref, g_ref, sh_ref, sc_ref, w_ref, qg_ref, kg_ref, cos_ref, sin_ref,
                q_ref, k_ref, v_ref, *, n_heads, n_kv, n_split):
    qg = qg_ref[...] * Q_SCALE
    kg = kg_ref[...]
    k0 = n_heads * HEAD_DIM
    rs = x_ref.shape[0] // n_split
    lane = lax.broadcasted_iota(jnp.int32, (rs, HEAD_DIM), 1)
    first_half = (lane % AXIS_DIM) < (AXIS_DIM // 2)
    for r in range(n_split):
        rows = slice(r * rs, (r + 1) * rs)
        h = _mod_norm(x_ref[rows, :], g_ref[...], sh_ref[0], sc_ref[0]).astype(BF16)
        qkv = jnp.dot(h, w_ref[...], preferred_element_type=F32)
        cos = cos_ref[rows, :]
        sin = sin_ref[rows, :]
        for hd in range(n_heads):
            cols = slice(hd * HEAD_DIM, (hd + 1) * HEAD_DIM)
            q_ref[rows, cols] = _head_norm_rope(qkv[:, cols], qg, cos, sin, first_half).astype(BF16)
        for hd in range(n_kv):
            cols = slice(k0 + hd * HEAD_DIM, k0 + (hd + 1) * HEAD_DIM)
            k_ref[rows, hd * HEAD_DIM:(hd + 1) * HEAD_DIM] = _head_norm_rope(
                qkv[:, cols], kg, cos, sin, first_half).astype(BF16)
        v_ref[rows, :] = qkv[:, k0 + n_kv * HEAD_DIM:].astype(BF16)


def _qkv(x, gain, mod, sub, w_qkv, q_g, k_g, cos, sin, seg_rows, seq_lens, group_rows, tm=512, n_split=2):
    t, d = x.shape
    n_tot = w_qkv.shape[1] // HEAD_DIM
    n_heads = d // HEAD_DIM
    n_kv = (n_tot - n_heads) // 2

    def pos_block(i):
        row0 = i * tm
        pos0 = jnp.where(row0 < group_rows, row0 % seq_lens[0], (row0 - group_rows) % seq_lens[1])
        return (pos0 // tm, 0)

    kern = functools.partial(_qkv_kernel, n_heads=n_heads, n_kv=n_kv, n_split=n_split)
    return pl.pallas_call(
        kern,
        grid=(t // tm,),
        in_specs=[
            pl.BlockSpec((tm, d), lambda i: (i, 0)),
            pl.BlockSpec((1, d), lambda i: (0, 0)),
            *_mod_specs(tm, seg_rows, sub, d, 1)[:2],
            pl.BlockSpec(w_qkv.shape, lambda i: (0, 0)),
            pl.BlockSpec((1, HEAD_DIM), lambda i: (0, 0)),
            pl.BlockSpec((1, HEAD_DIM), lambda i: (0, 0)),
            pl.BlockSpec((tm, HEAD_DIM), pos_block),
            pl.BlockSpec((tm, HEAD_DIM), pos_block),
        ],
        out_specs=[
            pl.BlockSpec((tm, n_heads * HEAD_DIM), lambda i: (i, 0)),
            pl.BlockSpec((tm, n_kv * HEAD_DIM), lambda i: (i, 0)),
            pl.BlockSpec((tm, n_kv * HEAD_DIM), lambda i: (i, 0)),
        ],
        out_shape=[
            jax.ShapeDtypeStruct((t, n_heads * HEAD_DIM), BF16),
            jax.ShapeDtypeStruct((t, n_kv * HEAD_DIM), BF16),
            jax.ShapeDtypeStruct((t, n_kv * HEAD_DIM), BF16),
        ],
        compiler_params=_params(("arbitrary",)),
        name="qkv_norm_rope",
    )(x, gain.reshape(1, d), mod, mod, w_qkv, q_g.reshape(1, HEAD_DIM), k_g.reshape(1, HEAD_DIM), cos, sin)


def _attend(q_ref, kv_refs, o_ref):
    chunks = []
    for k_ref, v_ref in kv_refs:
        vs = v_ref[...]
        chunks.append((k_ref[...], jnp.concatenate([vs, jnp.ones_like(vs)], axis=-1)))
    nblk = chunks[0][0].shape[0] // HEAD_DIM
    for g in range(GQA_GROUP):
        cols = slice(g * HEAD_DIM, (g + 1) * HEAD_DIM)
        q = q_ref[:, cols]
        m = acc = None
        for ks, v_ext in chunks:
            s = lax.dot_general(q, ks, (((1,), (1,)), ((), ())), preferred_element_type=F32)
            blocks = [s[:, c * HEAD_DIM:(c + 1) * HEAD_DIM] for c in range(nblk)]
            m_cur = jnp.max(functools.reduce(jnp.maximum, blocks), axis=-1, keepdims=True)
            m_new = jnp.broadcast_to(m_cur, blocks[0].shape) if m is None else jnp.maximum(m, m_cur)
            p = jnp.concatenate([jnp.exp2(blk - m_new) for blk in blocks], axis=-1).astype(BF16)
            pv = jnp.dot(p, v_ext, preferred_element_type=F32)
            if m is None:
                acc = pv
            else:
                alpha = jnp.exp2(m - m_new)
                acc = jnp.concatenate([alpha, alpha], axis=-1) * acc + pv
            m = m_new
        o_ref[:, cols] = (acc[:, :HEAD_DIM] / acc[:, HEAD_DIM:]).astype(o_ref.dtype)


def _flash_kernel(q_ref, *refs, tq, group_rows, n_chunks):
    o_ref = refs[-1]
    kv = [(refs[2 * c], refs[2 * c + 1]) for c in range(max(n_chunks))]
    if n_chunks[0] == n_chunks[1]:
        _attend(q_ref, kv, o_ref)
        return
    in_first = pl.program_id(1) * tq < group_rows

    @pl.when(in_first)
    def _():
        _attend(q_ref, kv[:n_chunks[0]], o_ref)

    @pl.when(jnp.logical_not(in_first))
    def _():
        _attend(q_ref, kv[:n_chunks[1]], o_ref)


def _flash(q, k, v, seq_lens, group_rows, tq=512):
    t, dq = q.shape
    n_kv = k.shape[1] // HEAD_DIM
    qw = GQA_GROUP * HEAD_DIM
    tk = math.gcd(*seq_lens)
    n_chunks = tuple(s // tk for s in seq_lens)

    def kv_spec(c):
        def imap(h, i):
            row0 = i * tq
            in_first = row0 < group_rows
            start = jnp.where(in_first, row0 // seq_lens[0] * seq_lens[0],
                              group_rows + (row0 - group_rows) // seq_lens[1] * seq_lens[1])
            off = jnp.where(c < jnp.where(in_first, n_chunks[0], n_chunks[1]), c, 0)
            return (start // tk + off, h)
        return pl.BlockSpec((tk, HEAD_DIM), imap)

    kv_specs, kv_args = [], []
    for c in range(max(n_chunks)):
        kv_specs += [kv_spec(c), kv_spec(c)]
        kv_args += [k, v]
    kern = functools.partial(_flash_kernel, tq=tq, group_rows=group_rows, n_chunks=n_chunks)
    return pl.pallas_call(
        kern,
        grid=(n_kv, t // tq),
        in_specs=[pl.BlockSpec((tq, qw), lambda h, i: (i, h)), *kv_specs],
        out_specs=pl.BlockSpec((tq, qw), lambda h, i: (i, h)),
        out_shape=jax.ShapeDtypeStruct((t, dq), BF16),
        compiler_params=_params(("arbitrary", "arbitrary")),
        name="flash_attention",
    )(q, *kv_args)


def _oproj_kernel(x_ref, a_ref, gt_ref, w_ref, o_ref):
    m = jnp.dot(a_ref[...], w_ref[...], preferred_element_type=F32)
    o_ref[...] = x_ref[...] + gt_ref[0] * m


def _oproj(x, attn, mod, sub, w_o, seg_rows, tm=512):
    t, d = x.shape
    return pl.pallas_call(
        _oproj_kernel,
        grid=(t // tm,),
        in_specs=[
            pl.BlockSpec((tm, d), lambda i: (i, 0)),
            pl.BlockSpec((tm, attn.shape[1]), lambda i: (i, 0)),
            _mod_specs(tm, seg_rows, sub, d, 1)[2],
            pl.BlockSpec(w_o.shape, lambda i: (0, 0)),
        ],
        out_specs=pl.BlockSpec((tm, d), lambda i: (i, 0)),
        out_shape=jax.ShapeDtypeStruct((t, d), F32),
        compiler_params=_params(("arbitrary",)),
        name="attn_out_proj",
    )(x, attn, mod, w_o)


def kernel(x_prompt, x_sample, c_prompt, c_sample, ada_w, ada_b, norm_g, ffn_w_in, ffn_w_out,
           pool_w, pool_scale, attn_w_qkv, attn_q_g, attn_k_g, attn_w_o):
    bp, sp, d = x_prompt.shape
    bs, ss, _ = x_sample.shape
    depth = ada_w.shape[0]
    group_rows = bp * sp
    seq_lens = (sp, ss)
    seg_rows = math.gcd(sp, ss)
    assert group_rows % seg_rows == 0 and sp % GRID_W == 0 and ss % GRID_W == 0

    x = jnp.concatenate([x_prompt.reshape(bp * sp, d), x_sample.reshape(bs * ss, d)], axis=0)
    seg_seq = [b for b in range(bp) for _ in range(sp // seg_rows)] + \
              [bp + b for b in range(bs) for _ in range(ss // seg_rows)]
    c_seg = jnp.concatenate([c_prompt, c_sample], axis=0)[jnp.array(seg_seq)]
    nseg = len(seg_seq)
    mod_all = _ada_mod(c_seg, ada_w, ada_b)
    cos, sin = _rope_tables(max(sp, ss))

    w_in = ffn_w_in.astype(BF16)
    w_out = ffn_w_out.astype(BF16)
    pool_wb = pool_w.astype(BF16)
    w_qkv = attn_w_qkv.astype(BF16)
    w_o = attn_w_o.astype(BF16)

    for i in range(depth):
        mod = mod_all[i].reshape(nseg * N_MOD_ROWS, 1, d)
        j = i // 2
        x = _ffn(x, norm_g[i, 0], mod, 0, w_in, w_out, i, 0, seg_rows)
        if i % 2 == 0:
            x = _pool(x, norm_g[i, 1], mod, 1, pool_wb[j], pool_scale[j], seg_rows, seq_lens, group_rows)
        else:
            q, k, v = _qkv(x, norm_g[i, 1], mod, 1, w_qkv[j], attn_q_g[j], attn_k_g[j], cos, sin,
                           seg_rows, seq_lens, group_rows)
            a = _flash(q, k, v, seq_lens, group_rows)
            x = _oproj(x, a, mod, 1, w_o[j], seg_rows)
        x = _ffn(x, norm_g[i, 2], mod, 2, w_in, w_out, i, 1, seg_rows)

    y_prompt = x[:group_rows].reshape(bp, sp, d)
    y_sample = x[group_rows:].reshape(bs, ss, d)
    return (y_prompt, y_sample)
```

```python
import functools
import math

import jax
import jax.numpy as jnp
from jax import lax
from jax.experimental import pallas as pl
from jax.experimental.pallas import tpu as pltpu

F32 = jnp.float32
BF16 = jnp.bfloat16

EPS = 1e-6
HEAD_DIM = 128
AXIS_DIM = HEAD_DIM // 2
ROPE_THETA = 10000.0
GRID_W = 64
GQA_GROUP = 4
POOL_WINDOWS = (2, 4, 8, 16)
SUBLANES = 8
POOL_PRE = 2 * SUBLANES
POOL_POST = SUBLANES
Q_SCALE = math.log2(math.e) / math.sqrt(HEAD_DIM)
N_MOD_ROWS = 9
NORM_ROWS = 256
KV_CHUNK = 2048

VMEM_LIMIT_BYTES = 60 * 1024 * 1024


def _params(semantics):
    return pltpu.CompilerParams(dimension_semantics=semantics, vmem_limit_bytes=VMEM_LIMIT_BYTES)


def _mod_norm(x, g, shift, scl):
    ms = jnp.mean(x * x, axis=-1, keepdims=True)
    y = x * lax.rsqrt(ms + EPS)
    return (y * g) * (1.0 + scl) + shift


def _ada_kernel(c_ref, w_ref, b_ref, o_ref):
    c = c_ref[...]
    s = (c * jax.nn.sigmoid(c)).astype(BF16)
    w = w_ref[0].astype(BF16)
    o_ref[0] = jnp.dot(s, w, preferred_element_type=F32) + b_ref[0]


def _ada_mod(c_all, ada_w, ada_b, tn=1024):
    depth, d, n = ada_w.shape
    nseq = c_all.shape[0]
    return pl.pallas_call(
        _ada_kernel,
        grid=(depth, n // tn),
        in_specs=[
            pl.BlockSpec((nseq, d), lambda l, j: (0, 0)),
            pl.BlockSpec((1, d, tn), lambda l, j: (l, 0, j)),
            pl.BlockSpec((1, 1, tn), lambda l, j: (l, 0, j)),
        ],
        out_specs=pl.BlockSpec((1, nseq, tn), lambda l, j: (l, 0, j)),
        out_shape=jax.ShapeDtypeStruct((depth, nseq, n), F32),
        compiler_params=_params(("arbitrary", "arbitrary")),
        name="ada_mod",
    )(c_all, ada_w, ada_b.reshape(depth, 1, n))


def _mod_specs(tm, seq, sub, d, ngrid):
    seq_len, seq0 = seq
    specs = []
    for r in range(3):
        row = sub * 3 + r
        if ngrid == 1:
            imap = lambda i, row=row: ((seq0 + (i * tm) // seq_len) * N_MOD_ROWS + row, 0, 0)
        else:
            imap = lambda i, k, row=row: ((seq0 + (i * tm) // seq_len) * N_MOD_ROWS + row, 0, 0)
        specs.append(pl.BlockSpec((1, 1, d), imap))
    return specs


def _ffn_kernel(x_ref, g_ref, sh_ref, sc_ref, gt_ref, wg_ref, wu_ref, wo_ref, *rest, n_split, has_cast):
    if has_cast:
        src_ref, o_ref, cast_ref, h_ref, inv_ref = rest
        cast_ref[...] = src_ref[...].astype(BF16)
    else:
        o_ref, h_ref, inv_ref = rest
    k = pl.program_id(1)

    @pl.when(k == 0)
    def _():
        n_chunks = h_ref.shape[0] // NORM_ROWS
        gmod = g_ref[...] * (1.0 + sc_ref[0])
        shift = sh_ref[0]

        def stats(r, carry):
            rows = pl.ds(pl.multiple_of(r * NORM_ROWS, NORM_ROWS), NORM_ROWS)
            x = x_ref[rows, :]
            inv_ref[rows, :] = lax.rsqrt(jnp.mean(x * x, axis=-1, keepdims=True) + EPS)
            return carry

        def apply(r, carry):
            rows = pl.ds(pl.multiple_of(r * NORM_ROWS, NORM_ROWS), NORM_ROWS)
            h_ref[rows, :] = ((x_ref[rows, :] * inv_ref[rows, :]) * gmod + shift).astype(BF16)
            o_ref[rows, :] = jnp.zeros((NORM_ROWS, o_ref.shape[1]), F32)
            return carry

        lax.fori_loop(0, n_chunks, stats, 0)
        lax.fori_loop(0, n_chunks, apply, 0)

    rs = h_ref.shape[0] // n_split
    tn = wg_ref.shape[1]
    for r in range(n_split):
        rows = slice(r * rs, (r + 1) * rs)
        hb = h_ref[rows, :]
        g = jnp.dot(hb, wg_ref[...], preferred_element_type=F32)
        u = jnp.dot(hb, wu_ref[...], preferred_element_type=F32)
        a = (g * jax.nn.sigmoid(g) * u).astype(BF16)
        for c in range(o_ref.shape[1] // tn):
            cols = slice(c * tn, (c + 1) * tn)
            o_ref[rows, cols] += jnp.dot(a, wo_ref[:, cols], preferred_element_type=F32)

    @pl.when(k == pl.num_programs(1) - 1)
    def _():
        o_ref[...] = x_ref[...] + (0.5 * gt_ref[0]) * o_ref[...]


def _ffn(x, gain, mod, sub, w_in, w_out, seq, cast_src=None, tm=1024, tf=512, n_split=2):
    t, d = x.shape
    d_ff = w_out.shape[0]
    nk = d_ff // tf
    ni = t // tm
    in_specs = [
        pl.BlockSpec((tm, d), lambda i, k: (i, 0)),
        pl.BlockSpec((1, d), lambda i, k: (0, 0)),
        *_mod_specs(tm, seq, sub, d, 2),
        pl.BlockSpec((d, tf), lambda i, k: (0, k)),
        pl.BlockSpec((d, tf), lambda i, k: (0, k + nk)),
        pl.BlockSpec((tf, d), lambda i, k: (k, 0)),
    ]
    args = [x, gain.reshape(1, d), mod, mod, mod, w_in, w_in, w_out]
    out_specs = [pl.BlockSpec((tm, d), lambda i, k: (i, 0))]
    out_shape = [jax.ShapeDtypeStruct((t, d), F32)]
    if cast_src is not None:
        src, prefix, by_rows = cast_src
        rows, cols = src.shape[-2:]
        nr, nc = (ni, nk) if by_rows else (nk, ni)
        assert rows % nr == 0 and cols % nc == 0
        blk = (rows // nr, cols // nc)
        pick = (lambda i, k: (i, k)) if by_rows else (lambda i, k: (k, i))
        lead = (None,) * len(prefix)
        in_specs.append(pl.BlockSpec(lead + blk, lambda i, k: tuple(prefix) + pick(i, k)))
        args.append(src)
        out_specs.append(pl.BlockSpec(blk, pick))
        out_shape.append(jax.ShapeDtypeStruct((rows, cols), BF16))
    res = pl.pallas_call(
        functools.partial(_ffn_kernel, n_split=n_split, has_cast=cast_src is not None),
        grid=(ni, nk),
        in_specs=in_specs,
        out_specs=out_specs,
        out_shape=out_shape,
        scratch_shapes=[pltpu.VMEM((tm, d), BF16), pltpu.VMEM((tm, 1), F32)],
        compiler_params=_params(("arbitrary", "arbitrary")),
        name="ffn",
    )(*args)
    return res if cast_src is not None else (res[0], None)


def _pool_kernel(x_ref, xp_ref, xn_ref, g_ref, sh_ref, sc_ref, gt_ref, w_ref, ps_ref, o_ref,
                 hbuf, lev_a, lev_b, *, tp, seq_len):
    pos0 = (pl.program_id(0) * tp) % seq_len
    g = g_ref[...]
    sh = sh_ref[0]
    sc = sc_ref[0]
    x = x_ref[...]
    d = x.shape[-1]
    gc = d // len(POOL_WINDOWS)
    t0 = POOL_PRE
    span = tp + POOL_PRE

    hp = _mod_norm(xp_ref[...], g, sh, sc)
    hn = _mod_norm(xn_ref[...], g, sh, sc)
    hbuf[0:SUBLANES, :] = jnp.zeros((SUBLANES, d), F32)
    lev_a[0:SUBLANES, :] = jnp.zeros((SUBLANES, gc), F32)
    lev_b[0:SUBLANES, :] = jnp.zeros((SUBLANES, gc), F32)
    hbuf[SUBLANES:t0, :] = jnp.where(pos0 > 0, hp, 0.0)
    hbuf[t0:t0 + tp, :] = _mod_norm(x, g, sh, sc)
    hbuf[t0 + tp:, :] = jnp.where(pos0 + tp < seq_len, hn, 0.0)

    pos = pos0 + lax.broadcasted_iota(jnp.int32, (tp, 1), 0)
    outs = []
    for gi, w in enumerate(POOL_WINDOWS):
        cols = slice(gi * gc, (gi + 1) * gc)
        src, dst, other = hbuf, lev_a, lev_b
        src_cols = cols
        step = 1
        while step < w:
            cur = src[SUBLANES:SUBLANES + span, src_cols] + src[SUBLANES - step:SUBLANES - step + span, src_cols]
            step *= 2
            if step < w or w > 2:
                dst[SUBLANES:SUBLANES + span, :] = cur
                src, dst, other = dst, other, dst
                src_cols = slice(None)
        if w == 2:
            win = cur[t0 - SUBLANES:t0 - SUBLANES + tp, :]
        else:
            end = t0 + w // 2 - 1
            win = src[end:end + tp, :]
        lo = jnp.maximum(pos - w // 2, 0)
        hi = jnp.minimum(pos + w // 2, seq_len)
        cnt = (hi - lo).astype(F32)
        p = win / cnt - hbuf[t0:t0 + tp, cols]
        outs.append(jnp.dot(p.astype(BF16), w_ref[gi], preferred_element_type=F32))
    m = jnp.concatenate(outs, axis=-1) * ps_ref[...]
    o_ref[...] = x + gt_ref[0] * m


def _pool(x, gain, mod, sub, pool_w, pool_scale, seq, tp=256):
    t, d = x.shape
    hb = tp // SUBLANES
    nhalo = t // SUBLANES
    gc = d // len(POOL_WINDOWS)
    kern = functools.partial(_pool_kernel, tp=tp, seq_len=seq[0])
    rows = POOL_PRE + tp + POOL_POST
    return pl.pallas_call(
        kern,
        grid=(t // tp,),
        in_specs=[
            pl.BlockSpec((tp, d), lambda i: (i, 0)),
            pl.BlockSpec((SUBLANES, d), lambda i: (jnp.maximum(i * hb - 1, 0), 0)),
            pl.BlockSpec((SUBLANES, d), lambda i: (jnp.minimum((i + 1) * hb, nhalo - 1), 0)),
            pl.BlockSpec((1, d), lambda i: (0, 0)),
            *_mod_specs(tp, seq, sub, d, 1),
            pl.BlockSpec(pool_w.shape, lambda i: (0, 0, 0)),
            pl.BlockSpec((1, d), lambda i: (0, 0)),
        ],
        out_specs=pl.BlockSpec((tp, d), lambda i: (i, 0)),
        out_shape=jax.ShapeDtypeStruct((t, d), F32),
        scratch_shapes=[pltpu.VMEM((rows, d), F32), pltpu.VMEM((rows, gc), F32), pltpu.VMEM((rows, gc), F32)],
        compiler_params=_params(("arbitrary",)),
        name="pool_mixer",
    )(x, x, x, gain.reshape(1, d), mod, mod, mod, pool_w, pool_scale.reshape(1, d))


def _rope_tables(s):
    t = jnp.arange(s)
    r = (t // GRID_W).astype(F32)
    c = (t % GRID_W).astype(F32)
    inv = ROPE_THETA ** (-jnp.arange(0, AXIS_DIM, 2, dtype=F32) / AXIS_DIM)
    ang_r = r[:, None] * inv[None, :]
    ang_c = c[:, None] * inv[None, :]
    cos = jnp.concatenate([jnp.cos(ang_r)] * 2 + [jnp.cos(ang_c)] * 2, axis=-1)
    sin = jnp.concatenate([-jnp.sin(ang_r), jnp.sin(ang_r), -jnp.sin(ang_c), jnp.sin(ang_c)], axis=-1)
    return cos, sin


def _head_norm_rope(xh, gain, cos, sin, first_half):
    ms = jnp.mean(xh * xh, axis=-1, keepdims=True)
    y = xh * lax.rsqrt(ms + EPS) * gain
    half = AXIS_DIM // 2
    partner = jnp.where(first_half, pltpu.roll(y, HEAD_DIM - half, 1), pltpu.roll(y, half, 1))
    return y * cos + partner * sin


def _qkv_kernel(x_ref, g_ref, sh_ref, sc_ref, w_ref, qg_ref, kg_ref, cos_ref, sin_ref,
                q_ref, k_ref, v_ref, *, n_heads, n_kv, n_split):
    qg = qg_ref[...] * Q_SCALE
    kg = kg_ref[...]
    k0 = n_heads * HEAD_DIM
    rs = x_ref.shape[0] // n_split
    lane = lax.broadcasted_iota(jnp.int32, (rs, HEAD_DIM), 1)
    first_half = (lane % AXIS_DIM) < (AXIS_DIM // 2)
    for r in range(n_split):
        rows = slice(r * rs, (r + 1) * rs)
        h = _mod_norm(x_ref[rows, :], g_ref[...], sh_ref[0], sc_ref[0]).astype(BF16)
        qkv = jnp.dot(h, w_ref[...], preferred_element_type=F32)
        cos = cos_ref[rows, :]
        sin = sin_ref[rows, :]
        for hd in range(n_heads):
            cols = slice(hd * HEAD_DIM, (hd + 1) * HEAD_DIM)
            q_ref[rows, cols] = _head_norm_rope(qkv[:, cols], qg, cos, sin, first_half).astype(BF16)
        for hd in range(n_kv):
            cols = slice(k0 + hd * HEAD_DIM, k0 + (hd + 1) * HEAD_DIM)
            k_ref[rows, hd * HEAD_DIM:(hd + 1) * HEAD_DIM] = _head_norm_rope(
                qkv[:, cols], kg, cos, sin, first_half).astype(BF16)
        v_ref[rows, :] = qkv[:, k0 + n_kv * HEAD_DIM:].astype(BF16)


def _qkv(x, gain, mod, sub, w_qkv, q_g, k_g, cos, sin, seq, tm=512, n_split=2):
    t, d = x.shape
    n_tot = w_qkv.shape[1] // HEAD_DIM
    n_heads = d // HEAD_DIM
    n_kv = (n_tot - n_heads) // 2
    seq_len = seq[0]
    pos_block = lambda i: (((i * tm) % seq_len) // tm, 0)
    kern = functools.partial(_qkv_kernel, n_heads=n_heads, n_kv=n_kv, n_split=n_split)
    return pl.pallas_call(
        kern,
        grid=(t // tm,),
        in_specs=[
            pl.BlockSpec((tm, d), lambda i: (i, 0)),
            pl.BlockSpec((1, d), lambda i: (0, 0)),
            *_mod_specs(tm, seq, sub, d, 1)[:2],
            pl.BlockSpec(w_qkv.shape, lambda i: (0, 0)),
            pl.BlockSpec((1, HEAD_DIM), lambda i: (0, 0)),
            pl.BlockSpec((1, HEAD_DIM), lambda i: (0, 0)),
            pl.BlockSpec((tm, HEAD_DIM), pos_block),
            pl.BlockSpec((tm, HEAD_DIM), pos_block),
        ],
        out_specs=[
            pl.BlockSpec((tm, n_heads * HEAD_DIM), lambda i: (i, 0)),
            pl.BlockSpec((tm, n_kv * HEAD_DIM), lambda i: (i, 0)),
            pl.BlockSpec((tm, n_kv * HEAD_DIM), lambda i: (i, 0)),
        ],
        out_shape=[
            jax.ShapeDtypeStruct((t, n_heads * HEAD_DIM), BF16),
            jax.ShapeDtypeStruct((t, n_kv * HEAD_DIM), BF16),
            jax.ShapeDtypeStruct((t, n_kv * HEAD_DIM), BF16),
        ],
        compiler_params=_params(("arbitrary",)),
        name="qkv_norm_rope",
    )(x, gain.reshape(1, d), mod, mod, w_qkv, q_g.reshape(1, HEAD_DIM), k_g.reshape(1, HEAD_DIM), cos, sin)


def _flash_kernel(q_ref, *refs, n_row_split):
    o_ref = refs[-1]
    chunks = []
    for c in range(len(refs) // 2):
        vs = refs[2 * c + 1][...]
        chunks.append((refs[2 * c][...], jnp.concatenate([vs, jnp.ones_like(vs)], axis=-1)))
    nblk = chunks[0][0].shape[0] // HEAD_DIM
    rs = q_ref.shape[0] // n_row_split
    for g in range(GQA_GROUP):
        cols = slice(g * HEAD_DIM, (g + 1) * HEAD_DIM)
        for r in range(n_row_split):
            rows = slice(r * rs, (r + 1) * rs)
            q = q_ref[rows, cols]
            m = acc = None
            for ks, v_ext in chunks:
                s = lax.dot_general(q, ks, (((1,), (1,)), ((), ())), preferred_element_type=F32)
                blocks = [s[:, c * HEAD_DIM:(c + 1) * HEAD_DIM] for c in range(nblk)]
                m_cur = jnp.max(functools.reduce(jnp.maximum, blocks), axis=-1, keepdims=True)
                m_new = jnp.broadcast_to(m_cur, blocks[0].shape) if m is None else jnp.maximum(m, m_cur)
                p = jnp.concatenate([jnp.exp2(blk - m_new) for blk in blocks], axis=-1).astype(BF16)
                pv = jnp.dot(p, v_ext, preferred_element_type=F32)
                if m is None:
                    acc = pv
                else:
                    alpha = jnp.exp2(m - m_new)
                    acc = jnp.concatenate([alpha, alpha], axis=-1) * acc + pv
                m = m_new
            o_ref[rows, cols] = (acc[:, :HEAD_DIM] / acc[:, HEAD_DIM:]).astype(o_ref.dtype)


def _flash(q, k, v, seq_len, tq=512):
    t, dq = q.shape
    n_kv = k.shape[1] // HEAD_DIM
    qw = GQA_GROUP * HEAD_DIM
    tk = min(seq_len, KV_CHUNK)
    n_chunks = seq_len // tk
    nq = seq_len // tq
    kv_specs, kv_args = [], []
    for c in range(n_chunks):
        spec = pl.BlockSpec((tk, HEAD_DIM), lambda b, h, i, c=c: (b * n_chunks + c, h))
        kv_specs += [spec, spec]
        kv_args += [k, v]
    kern = functools.partial(_flash_kernel, n_row_split=1 if n_chunks > 1 else 2)
    return pl.pallas_call(
        kern,
        grid=(t // seq_len, n_kv, nq),
        in_specs=[pl.BlockSpec((tq, qw), lambda b, h, i: (b * nq + i, h)), *kv_specs],
        out_specs=pl.BlockSpec((tq, qw), lambda b, h, i: (b * nq + i, h)),
        out_shape=jax.ShapeDtypeStruct((t, dq), BF16),
        compiler_params=_params(("arbitrary", "arbitrary", "arbitrary")),
        name="flash_attention",
    )(q, *kv_args)


def _oproj_kernel(x_ref, a_ref, gt_ref, w_ref, o_ref):
    m = jnp.dot(a_ref[...], w_ref[...], preferred_element_type=F32)
    o_ref[...] = x_ref[...] + gt_ref[0] * m


def _oproj(x, attn, mod, sub, w_o, seq, tm=512):
    t, d = x.shape
    return pl.pallas_call(
        _oproj_kernel,
        grid=(t // tm,),
        in_specs=[
            pl.BlockSpec((tm, d), lambda i: (i, 0)),
            pl.BlockSpec((tm, attn.shape[1]), lambda i: (i, 0)),
            _mod_specs(tm, seq, sub, d, 1)[2],
            pl.BlockSpec(w_o.shape, lambda i: (0, 0)),
        ],
        out_specs=pl.BlockSpec((tm, d), lambda i: (i, 0)),
        out_shape=jax.ShapeDtypeStruct((t, d), F32),
        compiler_params=_params(("arbitrary",)),
        name="attn_out_proj",
    )(x, attn, mod, w_o)


def kernel(x_prompt, x_sample, c_prompt, c_sample, ada_w, ada_b, norm_g, ffn_w_in, ffn_w_out,
           pool_w, pool_scale, attn_w_qkv, attn_q_g, attn_k_g, attn_w_o):
    bp, sp, d = x_prompt.shape
    bs, ss, _ = x_sample.shape
    depth = ada_w.shape[0]
    assert sp % GRID_W == 0 and ss % GRID_W == 0
    nseq = bp + bs
    mod_all = _ada_mod(jnp.concatenate([c_prompt, c_sample], axis=0), ada_w, ada_b)
    cos, sin = _rope_tables(max(sp, ss))

    xs = [x_prompt.reshape(bp * sp, d), x_sample.reshape(bs * ss, d)]
    seqs = [(sp, 0), (ss, bp)]

    pool_wb = pool_w.astype(BF16)
    w_qkv = attn_w_qkv.astype(BF16)
    w_o = attn_w_o.astype(BF16)
    w_in_b = ffn_w_in[0, 0].astype(BF16)
    w_out_b = ffn_w_out[0, 0].astype(BF16)

    def ffn_stage(xs, li, fi, w_in_b, w_out_b, mod):
        nxt = li * 2 + fi + 1
        nli, nfi = divmod(nxt, 2)
        has_next = nli < depth
        casts = [(ffn_w_in, (nli, nfi), True), (ffn_w_out, (nli, nfi), False)] if has_next else [None, None]
        sub = 0 if fi == 0 else 2
        outs, made = [], []
        for x, seq, cast in zip(xs, seqs, casts):
            y, wb = _ffn(x, norm_g[li, sub], mod, sub, w_in_b, w_out_b, seq, cast_src=cast)
            outs.append(y)
            made.append(wb)
        return outs, made[0], made[1]

    for i in range(depth):
        mod = mod_all[i].reshape(nseq * N_MOD_ROWS, 1, d)
        j = i // 2
        xs, w_in_b, w_out_b = ffn_stage(xs, i, 0, w_in_b, w_out_b, mod)
        if i % 2 == 0:
            xs = [_pool(x, norm_g[i, 1], mod, 1, pool_wb[j], pool_scale[j], seq) for x, seq in zip(xs, seqs)]
        else:
            nxt = []
            for x, seq in zip(xs, seqs):
                q, k, v = _qkv(x, norm_g[i, 1], mod, 1, w_qkv[j], attn_q_g[j], attn_k_g[j], cos, sin, seq)
                a = _flash(q, k, v, seq[0])
                nxt.append(_oproj(x, a, mod, 1, w_o[j], seq))
            xs = nxt
        xs, w_in_b, w_out_b = ffn_stage(xs, i, 1, w_in_b, w_out_b, mod)

    return (xs[0].reshape(bp, sp, d), xs[1].reshape(bs, ss, d))
```

```python
import functools
import math

import jax
import jax.numpy as jnp
from jax import lax
from jax.experimental import pallas as pl
from jax.experimental.pallas import tpu as pltpu

F32 = jnp.float32
BF16 = jnp.bfloat16

EPS = 1e-6
HEAD_DIM = 128
AXIS_DIM = HEAD_DIM // 2
ROPE_THETA = 10000.0
GRID_W = 64
GQA_GROUP = 4
POOL_WINDOWS = (2, 4, 8, 16)
SUBLANES = 8
POOL_PRE = 2 * SUBLANES
POOL_POST = SUBLANES
Q_SCALE = math.log2(math.e) / math.sqrt(HEAD_DIM)
N_MOD_ROWS = 9
NORM_ROWS = 256
KV_CHUNK = 2048

VMEM_LIMIT_BYTES = 60 * 1024 * 1024


def _params(semantics):
    return pltpu.CompilerParams(dimension_semantics=semantics, vmem_limit_bytes=VMEM_LIMIT_BYTES)


def _mod_norm(x, g, shift, scl):
    ms = jnp.mean(x * x, axis=-1, keepdims=True)
    y = x * lax.rsqrt(ms + EPS)
    return (y * g) * (1.0 + scl) + shift


def _ada_kernel(c_ref, w_ref, b_ref, o_ref):
    c = c_ref[...]
    s = (c * jax.nn.sigmoid(c)).astype(BF16)
    w = w_ref[0].astype(BF16)
    o_ref[0] = jnp.dot(s, w, preferred_element_type=F32) + b_ref[0]


def _ada_mod(c_all, ada_w, ada_b, tn=1024):
    depth, d, n = ada_w.shape
    nseq = c_all.shape[0]
    return pl.pallas_call(
        _ada_kernel,
        grid=(depth, n // tn),
        in_specs=[
            pl.BlockSpec((nseq, d), lambda l, j: (0, 0)),
            pl.BlockSpec((1, d, tn), lambda l, j: (l, 0, j)),
            pl.BlockSpec((1, 1, tn), lambda l, j: (l, 0, j)),
        ],
        out_specs=pl.BlockSpec((1, nseq, tn), lambda l, j: (l, 0, j)),
        out_shape=jax.ShapeDtypeStruct((depth, nseq, n), F32),
        compiler_params=_params(("arbitrary", "arbitrary")),
        name="ada_mod",
    )(c_all, ada_w, ada_b.reshape(depth, 1, n))


def _mod_specs(tm, seq, sub, d, ngrid):
    seq_len, seq0 = seq
    specs = []
    for r in range(3):
        row = sub * 3 + r
        if ngrid == 1:
            imap = lambda i, row=row: ((seq0 + (i * tm) // seq_len) * N_MOD_ROWS + row, 0, 0)
        else:
            imap = lambda i, k, row=row: ((seq0 + (i * tm) // seq_len) * N_MOD_ROWS + row, 0, 0)
        specs.append(pl.BlockSpec((1, 1, d), imap))
    return specs


def _ffn_kernel(x_ref, g_ref, sh_ref, sc_ref, gt_ref, wg_ref, wu_ref, wo_ref, *rest, n_split, has_cast):
    if has_cast:
        src_ref, o_ref, cast_ref, h_ref, inv_ref = rest
        cast_ref[...] = src_ref[...].astype(BF16)
    else:
        o_ref, h_ref, inv_ref = rest
    k = pl.program_id(1)
    last = pl.num_programs(1) - 1

    def norm_tile():
        n_chunks = h_ref.shape[0] // NORM_ROWS
        gmod = g_ref[...] * (1.0 + sc_ref[0])
        shift = sh_ref[0]

        def stats(r, carry):
            rows = pl.ds(pl.multiple_of(r * NORM_ROWS, NORM_ROWS), NORM_ROWS)
            x = x_ref[rows, :]
            inv_ref[rows, :] = lax.rsqrt(jnp.mean(x * x, axis=-1, keepdims=True) + EPS)
            return carry

        def apply(r, carry):
            rows = pl.ds(pl.multiple_of(r * NORM_ROWS, NORM_ROWS), NORM_ROWS)
            h_ref[rows, :] = ((x_ref[rows, :] * inv_ref[rows, :]) * gmod + shift).astype(BF16)
            return carry

        lax.fori_loop(0, n_chunks, stats, 0)
        lax.fori_loop(0, n_chunks, apply, 0)

    def step(mode):
        rs = h_ref.shape[0] // n_split
        tn = wg_ref.shape[1]
        for r in range(n_split):
            rows = slice(r * rs, (r + 1) * rs)
            hb = h_ref[rows, :]
            g = jnp.dot(hb, wg_ref[...], preferred_element_type=F32)
            u = jnp.dot(hb, wu_ref[...], preferred_element_type=F32)
            a = (g * jax.nn.sigmoid(g) * u).astype(BF16)
            for c in range(o_ref.shape[1] // tn):
                cols = slice(c * tn, (c + 1) * tn)
                part = jnp.dot(a, wo_ref[:, cols], preferred_element_type=F32)
                if mode == "first":
                    o_ref[rows, cols] = part
                elif mode == "middle":
                    o_ref[rows, cols] += part
                else:
                    o_ref[rows, cols] = x_ref[rows, cols] + (0.5 * gt_ref[0][:, cols]) * (o_ref[rows, cols] + part)

    @pl.when(k == 0)
    def _():
        norm_tile()
        step("first")

    @pl.when(jnp.logical_and(k > 0, k < last))
    def _():
        step("middle")

    @pl.when(k == last)
    def _():
        step("last")


def _ffn(x, gain, mod, sub, w_in, w_out, seq, cast_src=None, tm=1024, tf=512, n_split=2):
    t, d = x.shape
    d_ff = w_out.shape[0]
    nk = d_ff // tf
    ni = t // tm
    assert nk >= 2
    in_specs = [
        pl.BlockSpec((tm, d), lambda i, k: (i, 0)),
        pl.BlockSpec((1, d), lambda i, k: (0, 0)),
        *_mod_specs(tm, seq, sub, d, 2),
        pl.BlockSpec((d, tf), lambda i, k: (0, k)),
        pl.BlockSpec((d, tf), lambda i, k: (0, k + nk)),
        pl.BlockSpec((tf, d), lambda i, k: (k, 0)),
    ]
    args = [x, gain.reshape(1, d), mod, mod, mod, w_in, w_in, w_out]
    out_specs = [pl.BlockSpec((tm, d), lambda i, k: (i, 0))]
    out_shape = [jax.ShapeDtypeStruct((t, d), F32)]
    if cast_src is not None:
        src, prefix, by_rows = cast_src
        rows, cols = src.shape[-2:]
        nr, nc = (ni, nk) if by_rows else (nk, ni)
        assert rows % nr == 0 and cols % nc == 0
        blk = (rows // nr, cols // nc)
        pick = (lambda i, k: (i, k)) if by_rows else (lambda i, k: (k, i))
        lead = (None,) * len(prefix)
        in_specs.append(pl.BlockSpec(lead + blk, lambda i, k: tuple(prefix) + pick(i, k)))
        args.append(src)
        out_specs.append(pl.BlockSpec(blk, pick))
        out_shape.append(jax.ShapeDtypeStruct((rows, cols), BF16))
    res = pl.pallas_call(
        functools.partial(_ffn_kernel, n_split=n_split, has_cast=cast_src is not None),
        grid=(ni, nk),
        in_specs=in_specs,
        out_specs=out_specs,
        out_shape=out_shape,
        scratch_shapes=[pltpu.VMEM((tm, d), BF16), pltpu.VMEM((tm, 1), F32)],
        compiler_params=_params(("arbitrary", "arbitrary")),
        name="ffn",
    )(*args)
    return res if cast_src is not None else (res[0], None)


def _pool_kernel(x_ref, xp_ref, xn_ref, g_ref, sh_ref, sc_ref, gt_ref, w_ref, ps_ref, o_ref,
                 hbuf, lev_a, lev_b, *, tp, seq_len):
    pos0 = (pl.program_id(0) * tp) % seq_len
    g = g_ref[...]
    sh = sh_ref[0]
    sc = sc_ref[0]
    x = x_ref[...]
    d = x.shape[-1]
    gc = d // len(POOL_WINDOWS)
    t0 = POOL_PRE
    span = tp + POOL_PRE

    hp = _mod_norm(xp_ref[...], g, sh, sc)
    hn = _mod_norm(xn_ref[...], g, sh, sc)
    hbuf[0:SUBLANES, :] = jnp.zeros((SUBLANES, d), F32)
    lev_a[0:SUBLANES, :] = jnp.zeros((SUBLANES, gc), F32)
    lev_b[0:SUBLANES, :] = jnp.zeros((SUBLANES, gc), F32)
    hbuf[SUBLANES:t0, :] = jnp.where(pos0 > 0, hp, 0.0)
    hbuf[t0:t0 + tp, :] = _mod_norm(x, g, sh, sc)
    hbuf[t0 + tp:, :] = jnp.where(pos0 + tp < seq_len, hn, 0.0)

    pos = pos0 + lax.broadcasted_iota(jnp.int32, (tp, 1), 0)
    outs = []
    for gi, w in enumerate(POOL_WINDOWS):
        cols = slice(gi * gc, (gi + 1) * gc)
        src, dst, other = hbuf, lev_a, lev_b
        src_cols = cols
        step = 1
        while step < w:
            cur = src[SUBLANES:SUBLANES + span, src_cols] + src[SUBLANES - step:SUBLANES - step + span, src_cols]
            step *= 2
            if step < w or w > 2:
                dst[SUBLANES:SUBLANES + span, :] = cur
                src, dst, other = dst, other, dst
                src_cols = slice(None)
        if w == 2:
            win = cur[t0 - SUBLANES:t0 - SUBLANES + tp, :]
        else:
            end = t0 + w // 2 - 1
            win = src[end:end + tp, :]
        lo = jnp.maximum(pos - w // 2, 0)
        hi = jnp.minimum(pos + w // 2, seq_len)
        cnt = (hi - lo).astype(F32)
        p = win / cnt - hbuf[t0:t0 + tp, cols]
        outs.append(jnp.dot(p.astype(BF16), w_ref[gi], preferred_element_type=F32))
    m = jnp.concatenate(outs, axis=-1) * ps_ref[...]
    o_ref[...] = x + gt_ref[0] * m


def _pool(x, gain, mod, sub, pool_w, pool_scale, seq, tp=256):
    t, d = x.shape
    hb = tp // SUBLANES
    nhalo = t // SUBLANES
    gc = d // len(POOL_WINDOWS)
    kern = functools.partial(_pool_kernel, tp=tp, seq_len=seq[0])
    rows = POOL_PRE + tp + POOL_POST
    return pl.pallas_call(
        kern,
        grid=(t // tp,),
        in_specs=[
            pl.BlockSpec((tp, d), lambda i: (i, 0)),
            pl.BlockSpec((SUBLANES, d), lambda i: (jnp.maximum(i * hb - 1, 0), 0)),
            pl.BlockSpec((SUBLANES, d), lambda i: (jnp.minimum((i + 1) * hb, nhalo - 1), 0)),
            pl.BlockSpec((1, d), lambda i: (0, 0)),
            *_mod_specs(tp, seq, sub, d, 1),
            pl.BlockSpec(pool_w.shape, lambda i: (0, 0, 0)),
            pl.BlockSpec((1, d), lambda i: (0, 0)),
        ],
        out_specs=pl.BlockSpec((tp, d), lambda i: (i, 0)),
        out_shape=jax.ShapeDtypeStruct((t, d), F32),
        scratch_shapes=[pltpu.VMEM((rows, d), F32), pltpu.VMEM((rows, gc), F32), pltpu.VMEM((rows, gc), F32)],
        compiler_params=_params(("arbitrary",)),
        name="pool_mixer",
    )(x, x, x, gain.reshape(1, d), mod, mod, mod, pool_w, pool_scale.reshape(1, d))


def _rope_tables(s):
    t = jnp.arange(s)
    r = (t // GRID_W).astype(F32)
    c = (t % GRID_W).astype(F32)
    inv = ROPE_THETA ** (-jnp.arange(0, AXIS_DIM, 2, dtype=F32) / AXIS_DIM)
    ang_r = r[:, None] * inv[None, :]
    ang_c = c[:, None] * inv[None, :]
    cos = jnp.concatenate([jnp.cos(ang_r)] * 2 + [jnp.cos(ang_c)] * 2, axis=-1)
    sin = jnp.concatenate([-jnp.sin(ang_r), jnp.sin(ang_r), -jnp.sin(ang_c), jnp.sin(ang_c)], axis=-1)
    return cos, sin


def _head_norm_rope(xh, gain, cos, sin, first_half):
    ms = jnp.mean(xh * xh, axis=-1, keepdims=True)
    y = xh * lax.rsqrt(ms + EPS) * gain
    half = AXIS_DIM // 2
    partner = jnp.where(first_half, pltpu.roll(y, HEAD_DIM - half, 1), pltpu.roll(y, half, 1))
    return y * cos + partner * sin


def _qkv_kernel(x_ref, g_ref, sh_ref, sc_ref, w_ref, qg_ref, kg_ref, cos_ref, sin_ref,
                q_ref, k_ref, v_ref, *, n_heads, n_kv, n_split):
    qg = qg_ref[...] * Q_SCALE
    kg = kg_ref[...]
    k0 = n_heads * HEAD_DIM
    rs = x_ref.shape[0] // n_split
    lane = lax.broadcasted_iota(jnp.int32, (rs, HEAD_DIM), 1)
    first_half = (lane % AXIS_DIM) < (AXIS_DIM // 2)
    for r in range(n_split):
        rows = slice(r * rs, (r + 1) * rs)
        h = _mod_norm(x_ref[rows, :], g_ref[...], sh_ref[0], sc_ref[0]).astype(BF16)
        qkv = jnp.dot(h, w_ref[...], preferred_element_type=F32)
        cos = cos_ref[rows, :]
        sin = sin_ref[rows, :]
        for hd in range(n_heads):
            cols = slice(hd * HEAD_DIM, (hd + 1) * HEAD_DIM)
            q_ref[rows, cols] = _head_norm_rope(qkv[:, cols], qg, cos, sin, first_half).astype(BF16)
        for hd in range(n_kv):
            cols = slice(k0 + hd * HEAD_DIM, k0 + (hd + 1) * HEAD_DIM)
            k_ref[rows, hd * HEAD_DIM:(hd + 1) * HEAD_DIM] = _head_norm_rope(
                qkv[:, cols], kg, cos, sin, first_half).astype(BF16)
        v_ref[rows, :] = qkv[:, k0 + n_kv * HEAD_DIM:].astype(BF16)


def _qkv(x, gain, mod, sub, w_qkv, q_g, k_g, cos, sin, seq, tm=512, n_split=2):
    t, d = x.shape
    n_tot = w_qkv.shape[1] // HEAD_DIM
    n_heads = d // HEAD_DIM
    n_kv = (n_tot - n_heads) // 2
    seq_len = seq[0]
    pos_block = lambda i: (((i * tm) % seq_len) // tm, 0)
    kern = functools.partial(_qkv_kernel, n_heads=n_heads, n_kv=n_kv, n_split=n_split)
    return pl.pallas_call(
        kern,
        grid=(t // tm,),
        in_specs=[
            pl.BlockSpec((tm, d), lambda i: (i, 0)),
            pl.BlockSpec((1, d), lambda i: (0, 0)),
            *_mod_specs(tm, seq, sub, d, 1)[:2],
            pl.BlockSpec(w_qkv.shape, lambda i: (0, 0)),
            pl.BlockSpec((1, HEAD_DIM), lambda i: (0, 0)),
            pl.BlockSpec((1, HEAD_DIM), lambda i: (0, 0)),
            pl.BlockSpec((tm, HEAD_DIM), pos_block),
            pl.BlockSpec((tm, HEAD_DIM), pos_block),
        ],
        out_specs=[
            pl.BlockSpec((tm, n_heads * HEAD_DIM), lambda i: (i, 0)),
            pl.BlockSpec((tm, n_kv * HEAD_DIM), lambda i: (i, 0)),
            pl.BlockSpec((tm, n_kv * HEAD_DIM), lambda i: (i, 0)),
        ],
        out_shape=[
            jax.ShapeDtypeStruct((t, n_heads * HEAD_DIM), BF16),
            jax.ShapeDtypeStruct((t, n_kv * HEAD_DIM), BF16),
            jax.ShapeDtypeStruct((t, n_kv * HEAD_DIM), BF16),
        ],
        compiler_params=_params(("arbitrary",)),
        name="qkv_norm_rope",
    )(x, gain.reshape(1, d), mod, mod, w_qkv, q_g.reshape(1, HEAD_DIM), k_g.reshape(1, HEAD_DIM), cos, sin)


def _flash_kernel(q_ref, *refs, n_row_split):
    o_ref = refs[-1]
    chunks = []
    for c in range(len(refs) // 2):
        vs = refs[2 * c + 1][...]
        chunks.append((refs[2 * c][...], jnp.concatenate([vs, jnp.ones_like(vs)], axis=-1)))
    nblk = chunks[0][0].shape[0] // HEAD_DIM
    rs = q_ref.shape[0] // n_row_split
    for g in range(GQA_GROUP):
        cols = slice(g * HEAD_DIM, (g + 1) * HEAD_DIM)
        for r in range(n_row_split):
            rows = slice(r * rs, (r + 1) * rs)
            q = q_ref[rows, cols]
            m = acc = None
            for ks, v_ext in chunks:
                s = lax.dot_general(q, ks, (((1,), (1,)), ((), ())), preferred_element_type=F32)
                blocks = [s[:, c * HEAD_DIM:(c + 1) * HEAD_DIM] for c in range(nblk)]
                m_cur = jnp.max(functools.reduce(jnp.maximum, blocks), axis=-1, keepdims=True)
                m_new = jnp.broadcast_to(m_cur, blocks[0].shape) if m is None else jnp.maximum(m, m_cur)
                p = jnp.concatenate([jnp.exp2(blk - m_new) for blk in blocks], axis=-1).astype(BF16)
                pv = jnp.dot(p, v_ext, preferred_element_type=F32)
                if m is None:
                    acc = pv
                else:
                    alpha = jnp.exp2(m - m_new)
                    acc = jnp.concatenate([alpha, alpha], axis=-1) * acc + pv
                m = m_new
            o_ref[rows, cols] = (acc[:, :HEAD_DIM] / acc[:, HEAD_DIM:]).astype(o_ref.dtype)


def _flash(q, k, v, seq_len, tq=512, n_row_split=4):
    t, dq = q.shape
    n_kv = k.shape[1] // HEAD_DIM
    qw = GQA_GROUP * HEAD_DIM
    tk = min(seq_len, KV_CHUNK)
    n_chunks = seq_len // tk
    nq = seq_len // tq
    kv_specs, kv_args = [], []
    for c in range(n_chunks):
        spec = pl.BlockSpec((tk, HEAD_DIM), lambda b, h, i, c=c: (b * n_chunks + c, h))
        kv_specs += [spec, spec]
        kv_args += [k, v]
    kern = functools.partial(_flash_kernel, n_row_split=n_row_split)
    return pl.pallas_call(
        kern,
        grid=(t // seq_len, n_kv, nq),
        in_specs=[pl.BlockSpec((tq, qw), lambda b, h, i: (b * nq + i, h)), *kv_specs],
        out_specs=pl.BlockSpec((tq, qw), lambda b, h, i: (b * nq + i, h)),
        out_shape=jax.ShapeDtypeStruct((t, dq), BF16),
        compiler_params=_params(("arbitrary", "arbitrary", "arbitrary")),
        name="flash_attention",
    )(q, *kv_args)


def _oproj_kernel(x_ref, a_ref, gt_ref, w_ref, o_ref):
    m = jnp.dot(a_ref[...], w_ref[...], preferred_element_type=F32)
    o_ref[...] = x_ref[...] + gt_ref[0] * m


def _oproj(x, attn, mod, sub, w_o, seq, tm=512):
    t, d = x.shape
    return pl.pallas_call(
        _oproj_kernel,
        grid=(t // tm,),
        in_specs=[
            pl.BlockSpec((tm, d), lambda i: (i, 0)),
            pl.BlockSpec((tm, attn.shape[1]), lambda i: (i, 0)),
            _mod_specs(tm, seq, sub, d, 1)[2],
            pl.BlockSpec(w_o.shape, lambda i: (0, 0)),
        ],
        out_specs=pl.BlockSpec((tm, d), lambda i: (i, 0)),
        out_shape=jax.ShapeDtypeStruct((t, d), F32),
        compiler_params=_params(("arbitrary",)),
        name="attn_out_proj",
    )(x, attn, mod, w_o)


def kernel(x_prompt, x_sample, c_prompt, c_sample, ada_w, ada_b, norm_g, ffn_w_in, ffn_w_out,
           pool_w, pool_scale, attn_w_qkv, attn_q_g, attn_k_g, attn_w_o):
    bp, sp, d = x_prompt.shape
    bs, ss, _ = x_sample.shape
    depth = ada_w.shape[0]
    assert sp % GRID_W == 0 and ss % GRID_W == 0
    nseq = bp + bs
    mod_all = _ada_mod(jnp.concatenate([c_prompt, c_sample], axis=0), ada_w, ada_b)
    cos, sin = _rope_tables(max(sp, ss))

    xs = [x_prompt.reshape(bp * sp, d), x_sample.reshape(bs * ss, d)]
    seqs = [(sp, 0), (ss, bp)]

    pool_wb = pool_w.astype(BF16)
    w_qkv = attn_w_qkv.astype(BF16)
    w_o = attn_w_o.astype(BF16)
    w_in_b = ffn_w_in[0, 0].astype(BF16)
    w_out_b = ffn_w_out[0, 0].astype(BF16)

    def ffn_stage(xs, li, fi, w_in_b, w_out_b, mod):
        nxt = li * 2 + fi + 1
        nli, nfi = divmod(nxt, 2)
        has_next = nli < depth
        casts = [(ffn_w_in, (nli, nfi), True), (ffn_w_out, (nli, nfi), False)] if has_next else [None, None]
        sub = 0 if fi == 0 else 2
        outs, made = [], []
        for x, seq, cast in zip(xs, seqs, casts):
            y, wb = _ffn(x, norm_g[li, sub], mod, sub, w_in_b, w_out_b, seq, cast_src=cast)
            outs.append(y)
            made.append(wb)
        return outs, made[0], made[1]

    for i in range(depth):
        mod = mod_all[i].reshape(nseq * N_MOD_ROWS, 1, d)
        j = i // 2
        xs, w_in_b, w_out_b = ffn_stage(xs, i, 0, w_in_b, w_out_b, mod)
        if i % 2 == 0:
            xs = [_pool(x, norm_g[i, 1], mod, 1, pool_wb[j], pool_scale[j], seq) for x, seq in zip(xs, seqs)]
        else:
            nxt = []
            for x, seq in zip(xs, seqs):
                q, k, v = _qkv(x, norm_g[i, 1], mod, 1, w_qkv[j], attn_q_g[j], attn_k_g[j], cos, sin, seq)
                a = _flash(q, k, v, seq[0])
                nxt.append(_oproj(x, a, mod, 1, w_o[j], seq))
            xs = nxt
        xs, w_in_b, w_out_b = ffn_stage(xs, i, 1, w_in_b, w_out_b, mod)

    return (xs[0].reshape(bp, sp, d), xs[1].reshape(bs, ss, d))
```

```python
import functools
import math

import jax
import jax.numpy as jnp
from jax import lax
from jax.experimental import pallas as pl
from jax.experimental.pallas import tpu as pltpu

F32 = jnp.float32
BF16 = jnp.bfloat16

EPS = 1e-6
HEAD_DIM = 128
AXIS_DIM = HEAD_DIM // 2
ROPE_THETA = 10000.0
GRID_W = 64
GQA_GROUP = 4
POOL_WINDOWS = (2, 4, 8, 16)
SUBLANES = 8
POOL_PRE = 2 * SUBLANES
POOL_POST = SUBLANES
Q_SCALE = math.log2(math.e) / math.sqrt(HEAD_DIM)
N_MOD_ROWS = 9
NORM_ROWS = 256
KV_CHUNK = 2048

VMEM_LIMIT_BYTES = 60 * 1024 * 1024


def _params(semantics):
    return pltpu.CompilerParams(dimension_semantics=semantics, vmem_limit_bytes=VMEM_LIMIT_BYTES)


def _mod_norm(x, g, shift, scl):
    ms = jnp.mean(x * x, axis=-1, keepdims=True)
    y = x * lax.rsqrt(ms + EPS)
    return (y * g) * (1.0 + scl) + shift


def _ada_kernel(c_ref, w_ref, b_ref, o_ref):
    c = c_ref[...]
    s = (c * jax.nn.sigmoid(c)).astype(BF16)
    w = w_ref[0].astype(BF16)
    o_ref[0] = jnp.dot(s, w, preferred_element_type=F32) + b_ref[0]


def _ada_mod(c_all, ada_w, ada_b, tn=1024):
    depth, d, n = ada_w.shape
    nseq = c_all.shape[0]
    return pl.pallas_call(
        _ada_kernel,
        grid=(depth, n // tn),
        in_specs=[
            pl.BlockSpec((nseq, d), lambda l, j: (0, 0)),
            pl.BlockSpec((1, d, tn), lambda l, j: (l, 0, j)),
            pl.BlockSpec((1, 1, tn), lambda l, j: (l, 0, j)),
        ],
        out_specs=pl.BlockSpec((1, nseq, tn), lambda l, j: (l, 0, j)),
        out_shape=jax.ShapeDtypeStruct((depth, nseq, n), F32),
        compiler_params=_params(("arbitrary", "arbitrary")),
        name="ada_mod",
    )(c_all, ada_w, ada_b.reshape(depth, 1, n))


def _mod_specs(tm, seq, sub, d, ngrid):
    seq_len, seq0 = seq
    specs = []
    for r in range(3):
        row = sub * 3 + r
        if ngrid == 1:
            imap = lambda i, row=row: ((seq0 + (i * tm) // seq_len) * N_MOD_ROWS + row, 0, 0)
        else:
            imap = lambda i, k, row=row: ((seq0 + (i * tm) // seq_len) * N_MOD_ROWS + row, 0, 0)
        specs.append(pl.BlockSpec((1, 1, d), imap))
    return specs


def _ffn_kernel(x_ref, g_ref, sh_ref, sc_ref, gt_ref, wg_ref, wu_ref, wo_ref, *rest, n_split, has_cast):
    if has_cast:
        src_ref, o_ref, cast_ref, h_ref, inv_ref = rest
        cast_ref[...] = src_ref[...].astype(BF16)
    else:
        o_ref, h_ref, inv_ref = rest
    k = pl.program_id(1)
    last = pl.num_programs(1) - 1

    def norm_rows(r0, n):
        gmod = g_ref[...] * (1.0 + sc_ref[0])
        shift = sh_ref[0]
        for c0 in range(r0, r0 + n, NORM_ROWS):
            rows = slice(c0, c0 + NORM_ROWS)
            x = x_ref[rows, :]
            inv_ref[rows, :] = lax.rsqrt(jnp.mean(x * x, axis=-1, keepdims=True) + EPS)
        for c0 in range(r0, r0 + n, NORM_ROWS):
            rows = slice(c0, c0 + NORM_ROWS)
            h_ref[rows, :] = ((x_ref[rows, :] * inv_ref[rows, :]) * gmod + shift).astype(BF16)

    def step(mode):
        rs = h_ref.shape[0] // n_split
        tn = wg_ref.shape[1]
        for r in range(n_split):
            rows = slice(r * rs, (r + 1) * rs)
            if mode == "first":
                norm_rows(r * rs, rs)
            hb = h_ref[rows, :]
            g = jnp.dot(hb, wg_ref[...], preferred_element_type=F32)
            u = jnp.dot(hb, wu_ref[...], preferred_element_type=F32)
            a = (g * jax.nn.sigmoid(g) * u).astype(BF16)
            for c in range(o_ref.shape[1] // tn):
                cols = slice(c * tn, (c + 1) * tn)
                part = jnp.dot(a, wo_ref[:, cols], preferred_element_type=F32)
                if mode == "first":
                    o_ref[rows, cols] = part
                elif mode == "middle":
                    o_ref[rows, cols] += part
                else:
                    o_ref[rows, cols] = x_ref[rows, cols] + (0.5 * gt_ref[0][:, cols]) * (o_ref[rows, cols] + part)

    @pl.when(k == 0)
    def _():
        step("first")

    @pl.when(jnp.logical_and(k > 0, k < last))
    def _():
        step("middle")

    @pl.when(k == last)
    def _():
        step("last")


def _ffn(x, gain, mod, sub, w_in, w_out, seq, cast_src=None, tm=1024, tf=512, n_split=2):
    t, d = x.shape
    d_ff = w_out.shape[0]
    nk = d_ff // tf
    ni = t // tm
    assert nk >= 2
    in_specs = [
        pl.BlockSpec((tm, d), lambda i, k: (i, 0)),
        pl.BlockSpec((1, d), lambda i, k: (0, 0)),
        *_mod_specs(tm, seq, sub, d, 2),
        pl.BlockSpec((d, tf), lambda i, k: (0, k)),
        pl.BlockSpec((d, tf), lambda i, k: (0, k + nk)),
        pl.BlockSpec((tf, d), lambda i, k: (k, 0)),
    ]
    args = [x, gain.reshape(1, d), mod, mod, mod, w_in, w_in, w_out]
    out_specs = [pl.BlockSpec((tm, d), lambda i, k: (i, 0))]
    out_shape = [jax.ShapeDtypeStruct((t, d), F32)]
    if cast_src is not None:
        src, prefix, by_rows = cast_src
        rows, cols = src.shape[-2:]
        nr, nc = (ni, nk) if by_rows else (nk, ni)
        assert rows % nr == 0 and cols % nc == 0
        blk = (rows // nr, cols // nc)
        pick = (lambda i, k: (i, k)) if by_rows else (lambda i, k: (k, i))
        lead = (None,) * len(prefix)
        in_specs.append(pl.BlockSpec(lead + blk, lambda i, k: tuple(prefix) + pick(i, k)))
        args.append(src)
        out_specs.append(pl.BlockSpec(blk, pick))
        out_shape.append(jax.ShapeDtypeStruct((rows, cols), BF16))
    res = pl.pallas_call(
        functools.partial(_ffn_kernel, n_split=n_split, has_cast=cast_src is not None),
        grid=(ni, nk),
        in_specs=in_specs,
        out_specs=out_specs,
        out_shape=out_shape,
        scratch_shapes=[pltpu.VMEM((tm, d), BF16), pltpu.VMEM((tm, 1), F32)],
        compiler_params=_params(("arbitrary", "arbitrary")),
        name="ffn",
    )(*args)
    return res if cast_src is not None else (res[0], None)


def _pool_kernel(x_ref, xp_ref, xn_ref, g_ref, sh_ref, sc_ref, gt_ref, w_ref, ps_ref, o_ref,
                 hbuf, lev_a, lev_b, *, tp, seq_len):
    pos0 = (pl.program_id(0) * tp) % seq_len
    gmod = g_ref[...] * (1.0 + sc_ref[0])
    sh = sh_ref[0]
    x = x_ref[...]
    d = x.shape[-1]

    def norm(v):
        return (v * lax.rsqrt(jnp.mean(v * v, axis=-1, keepdims=True) + EPS)) * gmod + sh

    gc = d // len(POOL_WINDOWS)
    t0 = POOL_PRE
    span = tp + POOL_PRE

    hp = norm(xp_ref[...])
    hn = norm(xn_ref[...])
    hbuf[0:SUBLANES, :] = jnp.zeros((SUBLANES, d), F32)
    lev_a[0:SUBLANES, :] = jnp.zeros((SUBLANES, gc), F32)
    lev_b[0:SUBLANES, :] = jnp.zeros((SUBLANES, gc), F32)
    hbuf[SUBLANES:t0, :] = jnp.where(pos0 > 0, hp, 0.0)
    hbuf[t0:t0 + tp, :] = norm(x)
    hbuf[t0 + tp:, :] = jnp.where(pos0 + tp < seq_len, hn, 0.0)

    pos = pos0 + lax.broadcasted_iota(jnp.int32, (tp, 1), 0)
    outs = []
    for gi, w in enumerate(POOL_WINDOWS):
        cols = slice(gi * gc, (gi + 1) * gc)
        src, dst, other = hbuf, lev_a, lev_b
        src_cols = cols
        step = 1
        while step < w:
            cur = src[SUBLANES:SUBLANES + span, src_cols] + src[SUBLANES - step:SUBLANES - step + span, src_cols]
            step *= 2
            if step < w or w > 2:
                dst[SUBLANES:SUBLANES + span, :] = cur
                src, dst, other = dst, other, dst
                src_cols = slice(None)
        if w == 2:
            win = cur[t0 - SUBLANES:t0 - SUBLANES + tp, :]
        else:
            end = t0 + w // 2 - 1
            win = src[end:end + tp, :]
        lo = jnp.maximum(pos - w // 2, 0)
        hi = jnp.minimum(pos + w // 2, seq_len)
        cnt = (hi - lo).astype(F32)
        p = win / cnt - hbuf[t0:t0 + tp, cols]
        outs.append(jnp.dot(p.astype(BF16), w_ref[gi], preferred_element_type=F32))
    o_ref[...] = x + (gt_ref[0] * ps_ref[...]) * jnp.concatenate(outs, axis=-1)


def _pool(x, gain, mod, sub, pool_w, pool_scale, seq, tp=256):
    t, d = x.shape
    hb = tp // SUBLANES
    nhalo = t // SUBLANES
    gc = d // len(POOL_WINDOWS)
    kern = functools.partial(_pool_kernel, tp=tp, seq_len=seq[0])
    rows = POOL_PRE + tp + POOL_POST
    return pl.pallas_call(
        kern,
        grid=(t // tp,),
        in_specs=[
            pl.BlockSpec((tp, d), lambda i: (i, 0)),
            pl.BlockSpec((SUBLANES, d), lambda i: (jnp.maximum(i * hb - 1, 0), 0)),
            pl.BlockSpec((SUBLANES, d), lambda i: (jnp.minimum((i + 1) * hb, nhalo - 1), 0)),
            pl.BlockSpec((1, d), lambda i: (0, 0)),
            *_mod_specs(tp, seq, sub, d, 1),
            pl.BlockSpec(pool_w.shape, lambda i: (0, 0, 0)),
            pl.BlockSpec((1, d), lambda i: (0, 0)),
        ],
        out_specs=pl.BlockSpec((tp, d), lambda i: (i, 0)),
        out_shape=jax.ShapeDtypeStruct((t, d), F32),
        scratch_shapes=[pltpu.VMEM((rows, d), F32), pltpu.VMEM((rows, gc), F32), pltpu.VMEM((rows, gc), F32)],
        compiler_params=_params(("arbitrary",)),
        name="pool_mixer",
    )(x, x, x, gain.reshape(1, d), mod, mod, mod, pool_w, pool_scale.reshape(1, d))


def _rope_tables(s):
    t = jnp.arange(s)
    r = (t // GRID_W).astype(F32)
    c = (t % GRID_W).astype(F32)
    inv = ROPE_THETA ** (-jnp.arange(0, AXIS_DIM, 2, dtype=F32) / AXIS_DIM)
    ang_r = r[:, None] * inv[None, :]
    ang_c = c[:, None] * inv[None, :]
    cos = jnp.concatenate([jnp.cos(ang_r)] * 2 + [jnp.cos(ang_c)] * 2, axis=-1)
    sin = jnp.concatenate([-jnp.sin(ang_r), jnp.sin(ang_r), -jnp.sin(ang_c), jnp.sin(ang_c)], axis=-1)
    return cos, sin


def _head_norm_rope(xh, gain, cos, sin, first_half):
    ms = jnp.mean(xh * xh, axis=-1, keepdims=True)
    y = xh * lax.rsqrt(ms + EPS) * gain
    half = AXIS_DIM // 2
    partner = jnp.where(first_half, pltpu.roll(y, HEAD_DIM - half, 1), pltpu.roll(y, half, 1))
    return y * cos + partner * sin


def _qkv_kernel(x_ref, g_ref, sh_ref, sc_ref, w_ref, qg_ref, kg_ref, cos_ref, sin_ref,
                q_ref, k_ref, v_ref, *, n_heads, n_kv, n_split):
    qg = qg_ref[...] * Q_SCALE
    kg = kg_ref[...]
    k0 = n_heads * HEAD_DIM
    rs = x_ref.shape[0] // n_split
    lane = lax.broadcasted_iota(jnp.int32, (rs, HEAD_DIM), 1)
    first_half = (lane % AXIS_DIM) < (AXIS_DIM // 2)
    for r in range(n_split):
        rows = slice(r * rs, (r + 1) * rs)
        h = _mod_norm(x_ref[rows, :], g_ref[...], sh_ref[0], sc_ref[0]).astype(BF16)
        qkv = jnp.dot(h, w_ref[...], preferred_element_type=F32)
        cos = cos_ref[rows, :]
        sin = sin_ref[rows, :]
        for hd in range(n_heads):
            cols = slice(hd * HEAD_DIM, (hd + 1) * HEAD_DIM)
            q_ref[rows, cols] = _head_norm_rope(qkv[:, cols], qg, cos, sin, first_half).astype(BF16)
        for hd in range(n_kv):
            cols = slice(k0 + hd * HEAD_DIM, k0 + (hd + 1) * HEAD_DIM)
            k_ref[rows, hd * HEAD_DIM:(hd + 1) * HEAD_DIM] = _head_norm_rope(
                qkv[:, cols], kg, cos, sin, first_half).astype(BF16)
        v_ref[rows, :] = qkv[:, k0 + n_kv * HEAD_DIM:].astype(BF16)


def _qkv(x, gain, mod, sub, w_qkv, q_g, k_g, cos, sin, seq, tm=512, n_split=2):
    t, d = x.shape
    n_tot = w_qkv.shape[1] // HEAD_DIM
    n_heads = d // HEAD_DIM
    n_kv = (n_tot - n_heads) // 2
    seq_len = seq[0]
    pos_block = lambda i: (((i * tm) % seq_len) // tm, 0)
    kern = functools.partial(_qkv_kernel, n_heads=n_heads, n_kv=n_kv, n_split=n_split)
    return pl.pallas_call(
        kern,
        grid=(t // tm,),
        in_specs=[
            pl.BlockSpec((tm, d), lambda i: (i, 0)),
            pl.BlockSpec((1, d), lambda i: (0, 0)),
            *_mod_specs(tm, seq, sub, d, 1)[:2],
            pl.BlockSpec(w_qkv.shape, lambda i: (0, 0)),
            pl.BlockSpec((1, HEAD_DIM), lambda i: (0, 0)),
            pl.BlockSpec((1, HEAD_DIM), lambda i: (0, 0)),
            pl.BlockSpec((tm, HEAD_DIM), pos_block),
            pl.BlockSpec((tm, HEAD_DIM), pos_block),
        ],
        out_specs=[
            pl.BlockSpec((tm, n_heads * HEAD_DIM), lambda i: (i, 0)),
            pl.BlockSpec((tm, n_kv * HEAD_DIM), lambda i: (i, 0)),
            pl.BlockSpec((tm, n_kv * HEAD_DIM), lambda i: (i, 0)),
        ],
        out_shape=[
            jax.ShapeDtypeStruct((t, n_heads * HEAD_DIM), BF16),
            jax.ShapeDtypeStruct((t, n_kv * HEAD_DIM), BF16),
            jax.ShapeDtypeStruct((t, n_kv * HEAD_DIM), BF16),
        ],
        compiler_params=_params(("arbitrary",)),
        name="qkv_norm_rope",
    )(x, gain.reshape(1, d), mod, mod, w_qkv, q_g.reshape(1, HEAD_DIM), k_g.reshape(1, HEAD_DIM), cos, sin)


def _flash_kernel(q_ref, *refs, n_row_split):
    o_ref = refs[-1]
    chunks = []
    for c in range(len(refs) // 2):
        vs = refs[2 * c + 1][...]
        chunks.append((refs[2 * c][...], jnp.concatenate([vs, jnp.ones_like(vs)], axis=-1)))
    nblk = chunks[0][0].shape[0] // HEAD_DIM
    rs = q_ref.shape[0] // n_row_split
    for g in range(GQA_GROUP):
        cols = slice(g * HEAD_DIM, (g + 1) * HEAD_DIM)
        for r in range(n_row_split):
            rows = slice(r * rs, (r + 1) * rs)
            q = q_ref[rows, cols]
            m = acc = None
            for ks, v_ext in chunks:
                s = lax.dot_general(q, ks, (((1,), (1,)), ((), ())), preferred_element_type=F32)
                blocks = [s[:, c * HEAD_DIM:(c + 1) * HEAD_DIM] for c in range(nblk)]
                m_cur = jnp.max(functools.reduce(jnp.maximum, blocks), axis=-1, keepdims=True)
                m_new = jnp.broadcast_to(m_cur, blocks[0].shape) if m is None else jnp.maximum(m, m_cur)
                p = jnp.concatenate([jnp.exp2(blk - m_new) for blk in blocks], axis=-1).astype(BF16)
                pv = jnp.dot(p, v_ext, preferred_element_type=F32)
                if m is None:
                    acc = pv
                else:
                    alpha = jnp.exp2(m - m_new)
                    acc = jnp.concatenate([alpha, alpha], axis=-1) * acc + pv
                m = m_new
            o_ref[rows, cols] = (acc[:, :HEAD_DIM] / acc[:, HEAD_DIM:]).astype(o_ref.dtype)


def _flash(q, k, v, seq_len, tq=512, n_row_split=4):
    t, dq = q.shape
    n_kv = k.shape[1] // HEAD_DIM
    qw = GQA_GROUP * HEAD_DIM
    tk = min(seq_len, KV_CHUNK)
    n_chunks = seq_len // tk
    nq = seq_len // tq
    kv_specs, kv_args = [], []
    for c in range(n_chunks):
        spec = pl.BlockSpec((tk, HEAD_DIM), lambda b, h, i, c=c: (b * n_chunks + c, h))
        kv_specs += [spec, spec]
        kv_args += [k, v]
    kern = functools.partial(_flash_kernel, n_row_split=n_row_split)
    return pl.pallas_call(
        kern,
        grid=(t // seq_len, n_kv, nq),
        in_specs=[pl.BlockSpec((tq, qw), lambda b, h, i: (b * nq + i, h)), *kv_specs],
        out_specs=pl.BlockSpec((tq, qw), lambda b, h, i: (b * nq + i, h)),
        out_shape=jax.ShapeDtypeStruct((t, dq), BF16),
        compiler_params=_params(("arbitrary", "arbitrary", "arbitrary")),
        name="flash_attention",
    )(q, *kv_args)


def _oproj_kernel(x_ref, a_ref, gt_ref, w_ref, o_ref):
    m = jnp.dot(a_ref[...], w_ref[...], preferred_element_type=F32)
    o_ref[...] = x_ref[...] + gt_ref[0] * m


def _oproj(x, attn, mod, sub, w_o, seq, tm=512):
    t, d = x.shape
    return pl.pallas_call(
        _oproj_kernel,
        grid=(t // tm,),
        in_specs=[
            pl.BlockSpec((tm, d), lambda i: (i, 0)),
            pl.BlockSpec((tm, attn.shape[1]), lambda i: (i, 0)),
            _mod_specs(tm, seq, sub, d, 1)[2],
            pl.BlockSpec(w_o.shape, lambda i: (0, 0)),
        ],
        out_specs=pl.BlockSpec((tm, d), lambda i: (i, 0)),
        out_shape=jax.ShapeDtypeStruct((t, d), F32),
        compiler_params=_params(("arbitrary",)),
        name="attn_out_proj",
    )(x, attn, mod, w_o)


def kernel(x_prompt, x_sample, c_prompt, c_sample, ada_w, ada_b, norm_g, ffn_w_in, ffn_w_out,
           pool_w, pool_scale, attn_w_qkv, attn_q_g, attn_k_g, attn_w_o):
    bp, sp, d = x_prompt.shape
    bs, ss, _ = x_sample.shape
    depth = ada_w.shape[0]
    assert sp % GRID_W == 0 and ss % GRID_W == 0
    nseq = bp + bs
    mod_all = _ada_mod(jnp.concatenate([c_prompt, c_sample], axis=0), ada_w, ada_b)
    cos, sin = _rope_tables(max(sp, ss))

    xs = [x_prompt.reshape(bp * sp, d), x_sample.reshape(bs * ss, d)]
    seqs = [(sp, 0), (ss, bp)]

    pool_wb = pool_w.astype(BF16)
    w_qkv = attn_w_qkv.astype(BF16)
    w_o = attn_w_o.astype(BF16)
    w_in_b = ffn_w_in[0, 0].astype(BF16)
    w_out_b = ffn_w_out[0, 0].astype(BF16)

    def ffn_stage(xs, li, fi, w_in_b, w_out_b, mod):
        nxt = li * 2 + fi + 1
        nli, nfi = divmod(nxt, 2)
        has_next = nli < depth
        casts = [(ffn_w_in, (nli, nfi), True), (ffn_w_out, (nli, nfi), False)] if has_next else [None, None]
        sub = 0 if fi == 0 else 2
        outs, made = [], []
        for x, seq, cast in zip(xs, seqs, casts):
            y, wb = _ffn(x, norm_g[li, sub], mod, sub, w_in_b, w_out_b, seq, cast_src=cast)
            outs.append(y)
            made.append(wb)
        return outs, made[0], made[1]

    for i in range(depth):
        mod = mod_all[i].reshape(nseq * N_MOD_ROWS, 1, d)
        j = i // 2
        xs, w_in_b, w_out_b = ffn_stage(xs, i, 0, w_in_b, w_out_b, mod)
        if i % 2 == 0:
            xs = [_pool(x, norm_g[i, 1], mod, 1, pool_wb[j], pool_scale[j], seq) for x, seq in zip(xs, seqs)]
        else:
            nxt = []
            for x, seq in zip(xs, seqs):
                q, k, v = _qkv(x, norm_g[i, 1], mod, 1, w_qkv[j], attn_q_g[j], attn_k_g[j], cos, sin, seq)
                a = _flash(q, k, v, seq[0])
                nxt.append(_oproj(x, a, mod, 1, w_o[j], seq))
            xs = nxt
        xs, w_in_b, w_out_b = ffn_stage(xs, i, 1, w_in_b, w_out_b, mod)

    return (xs[0].reshape(bp, sp, d), xs[1].reshape(bs, ss, d))
```

```python
import functools
import math

import jax
import jax.numpy as jnp
from jax import lax
from jax.experimental import pallas as pl
from jax.experimental.pallas import tpu as pltpu

F32 = jnp.float32
BF16 = jnp.bfloat16

EPS = 1e-6
HEAD_DIM = 128
AXIS_DIM = HEAD_DIM // 2
ROPE_THETA = 10000.0
GRID_W = 64
GQA_GROUP = 4
POOL_WINDOWS = (2, 4, 8, 16)
SUBLANES = 8
POOL_PRE = 2 * SUBLANES
POOL_POST = SUBLANES
Q_SCALE = math.log2(math.e) / math.sqrt(HEAD_DIM)
N_MOD_ROWS = 9
NORM_ROWS = 256
KV_CHUNK = 2048
FFN_TM = 1024
FFN_TF = 512
SIDE_STEPS = 8

VMEM_LIMIT_BYTES = 60 * 1024 * 1024


def _params(semantics):
    return pltpu.CompilerParams(dimension_semantics=semantics, vmem_limit_bytes=VMEM_LIMIT_BYTES)


def _mod_norm(x, g, shift, scl):
    ms = jnp.mean(x * x, axis=-1, keepdims=True)
    y = x * lax.rsqrt(ms + EPS)
    return (y * g) * (1.0 + scl) + shift


def _ada_kernel(c_ref, w_ref, b_ref, o_ref):
    c = c_ref[...]
    s = (c * jax.nn.sigmoid(c)).astype(BF16)
    w = w_ref[0].astype(BF16)
    o_ref[0] = jnp.dot(s, w, preferred_element_type=F32) + b_ref[0]


def _ada_mod(c_all, ada_w, ada_b, first, count, tn=1024):
    depth, d, n = ada_w.shape
    nseq = c_all.shape[0]
    return pl.pallas_call(
        _ada_kernel,
        grid=(count, n // tn),
        in_specs=[
            pl.BlockSpec((nseq, d), lambda l, j: (0, 0)),
            pl.BlockSpec((1, d, tn), lambda l, j: (first + l, 0, j)),
            pl.BlockSpec((1, 1, tn), lambda l, j: (first + l, 0, j)),
        ],
        out_specs=pl.BlockSpec((1, nseq, tn), lambda l, j: (l, 0, j)),
        out_shape=jax.ShapeDtypeStruct((count, nseq, n), F32),
        compiler_params=_params(("arbitrary", "arbitrary")),
        name="ada_mod",
    )(c_all, ada_w, ada_b.reshape(depth, 1, n))


def _mod_specs(tm, seq, sub, d, ngrid):
    seq_len, seq0 = seq
    specs = []
    for r in range(3):
        row = sub * 3 + r
        if ngrid == 1:
            imap = lambda i, row=row: ((seq0 + (i * tm) // seq_len) * N_MOD_ROWS + row, 0, 0)
        else:
            imap = lambda i, k, row=row: ((seq0 + (i * tm) // seq_len) * N_MOD_ROWS + row, 0, 0)
        specs.append(pl.BlockSpec((1, 1, d), imap))
    return specs


def _cast_job(src, prefix, ni, nk, by_rows):
    rows, cols = src.shape[-2:]
    nr, nc = (ni, nk) if by_rows else (nk, ni)
    assert rows % nr == 0 and cols % nc == 0
    blk = (rows // nr, cols // nc)
    pick = (lambda i, k: (i, k)) if by_rows else (lambda i, k: (k, i))
    lead = (None,) * len(prefix)

    def body(ins, out, k):
        out[...] = ins[0][...].astype(BF16)

    return ([pl.BlockSpec(lead + blk, lambda i, k: tuple(prefix) + pick(i, k))], [src],
            pl.BlockSpec(blk, pick), jax.ShapeDtypeStruct((rows, cols), BF16), body)


def _cast_blocked_job(src, prefix, ni, nk, tf):
    rows, cols = src.shape[-2:]
    assert rows % ni == 0 and cols % (nk * tf) == 0
    per = cols // (nk * tf)
    lead = (None,) * len(prefix)

    def body(ins, out, k):
        for b in range(per):
            out[b] = ins[0][:, b * tf:(b + 1) * tf].astype(BF16)

    return ([pl.BlockSpec(lead + (rows // ni, per * tf), lambda i, k: tuple(prefix) + (i, k))], [src],
            pl.BlockSpec((per, rows // ni, tf), lambda i, k: (k, i, 0)),
            jax.ShapeDtypeStruct((cols // tf, rows, tf), BF16), body)


def _cast_part_job(src, prefix, ni, nk, kc):
    rows, cols = src.shape[-2:]
    assert kc <= nk and rows % ni == 0 and cols % kc == 0 and (cols // kc) % 128 == 0
    blk = (rows // ni, cols // kc)
    lead = (None,) * len(prefix)
    pick = lambda i, k: (i, jnp.minimum(k, kc - 1))

    def body(ins, out, k):
        @pl.when(k < kc)
        def _():
            out[...] = ins[0][...].astype(BF16)

    return ([pl.BlockSpec(lead + blk, lambda i, k: tuple(prefix) + pick(i, k))], [src],
            pl.BlockSpec(blk, pick), jax.ShapeDtypeStruct((rows, cols), BF16), body)


def _ada_job(c_all, ada_w, ada_b3, layer, part, n_parts, ni, nk):
    depth, d, n = ada_w.shape
    nseq = c_all.shape[0]
    width = n // n_parts
    assert n % n_parts == 0 and width % (128 * ni) == 0
    kc = width // (128 * ni)
    assert kc <= nk
    off = part * (width // 128)
    pick = lambda i, k: i * kc + jnp.minimum(k, kc - 1)

    def body(ins, out, k):
        @pl.when(k < kc)
        def _():
            c = ins[0][...]
            s = (c * jax.nn.sigmoid(c)).astype(BF16)
            out[...] = jnp.dot(s, ins[1][...].astype(BF16), preferred_element_type=F32) + ins[2][...]

    return ([pl.BlockSpec((nseq, d), lambda i, k: (0, 0)),
             pl.BlockSpec((None, d, 128), lambda i, k: (layer, 0, off + pick(i, k))),
             pl.BlockSpec((None, 1, 128), lambda i, k: (layer, 0, off + pick(i, k)))],
            [c_all, ada_w, ada_b3],
            pl.BlockSpec((nseq, 128), lambda i, k: (0, pick(i, k))),
            jax.ShapeDtypeStruct((nseq, width), F32), body)


def _ffn_kernel(x_ref, g_ref, sh_ref, sc_ref, gt_ref, wg_ref, wu_ref, wo_ref, *rest, n_split, sides):
    n_side_in = sum(n for n, _ in sides)
    side_ins, o_ref = rest[:n_side_in], rest[n_side_in]
    side_outs = rest[n_side_in + 1:n_side_in + 1 + len(sides)]
    h_ref, inv_ref = rest[n_side_in + 1 + len(sides):]
    k = pl.program_id(1)
    at = 0
    for (n, body), out in zip(sides, side_outs):
        body(side_ins[at:at + n], out, k)
        at += n
    last = pl.num_programs(1) - 1

    def norm_rows(r0, n):
        gmod = g_ref[...] * (1.0 + sc_ref[0])
        shift = sh_ref[0]
        for c0 in range(r0, r0 + n, NORM_ROWS):
            rows = slice(c0, c0 + NORM_ROWS)
            x = x_ref[rows, :]
            inv_ref[rows, :] = lax.rsqrt(jnp.mean(x * x, axis=-1, keepdims=True) + EPS)
        for c0 in range(r0, r0 + n, NORM_ROWS):
            rows = slice(c0, c0 + NORM_ROWS)
            h_ref[rows, :] = ((x_ref[rows, :] * inv_ref[rows, :]) * gmod + shift).astype(BF16)

    def step(mode):
        rs = h_ref.shape[0] // n_split
        tn = wg_ref.shape[1]
        for r in range(n_split):
            rows = slice(r * rs, (r + 1) * rs)
            if mode == "first":
                norm_rows(r * rs, rs)
            hb = h_ref[rows, :]
            g = jnp.dot(hb, wg_ref[...], preferred_element_type=F32)
            u = jnp.dot(hb, wu_ref[...], preferred_element_type=F32)
            a = (g * jax.nn.sigmoid(g) * u).astype(BF16)
            for c in range(o_ref.shape[1] // tn):
                cols = slice(c * tn, (c + 1) * tn)
                part = jnp.dot(a, wo_ref[:, cols], preferred_element_type=F32)
                if mode == "first":
                    o_ref[rows, cols] = part
                elif mode == "middle":
                    o_ref[rows, cols] += part
                else:
                    o_ref[rows, cols] = x_ref[rows, cols] + (0.5 * gt_ref[0][:, cols]) * (o_ref[rows, cols] + part)

    @pl.when(k == 0)
    def _():
        step("first")

    @pl.when(jnp.logical_and(k > 0, k < last))
    def _():
        step("middle")

    @pl.when(k == last)
    def _():
        step("last")


def _ffn(x, gain, mod, sub, w_in, w_out, seq, jobs=(), tm=1024, n_split=2):
    t, d = x.shape
    tf = w_in.shape[2]
    d_ff = w_out.shape[0]
    nk = d_ff // tf
    ni = t // tm
    assert nk >= 2
    in_specs = [
        pl.BlockSpec((tm, d), lambda i, k: (i, 0)),
        pl.BlockSpec((1, d), lambda i, k: (0, 0)),
        *_mod_specs(tm, seq, sub, d, 2),
        pl.BlockSpec((None, d, tf), lambda i, k: (k, 0, 0)),
        pl.BlockSpec((None, d, tf), lambda i, k: (k + nk, 0, 0)),
        pl.BlockSpec((tf, d), lambda i, k: (k, 0)),
    ]
    args = [x, gain.reshape(1, d), mod, mod, mod, w_in, w_in, w_out]
    out_specs = [pl.BlockSpec((tm, d), lambda i, k: (i, 0))]
    out_shape = [jax.ShapeDtypeStruct((t, d), F32)]
    sides = []
    for j_in_specs, j_args, j_out_spec, j_out_shape, body in jobs:
        in_specs += j_in_specs
        args += j_args
        out_specs.append(j_out_spec)
        out_shape.append(j_out_shape)
        sides.append((len(j_args), body))
    res = pl.pallas_call(
        functools.partial(_ffn_kernel, n_split=n_split, sides=tuple(sides)),
        grid=(ni, nk),
        in_specs=in_specs,
        out_specs=out_specs,
        out_shape=out_shape,
        scratch_shapes=[pltpu.VMEM((tm, d), BF16), pltpu.VMEM((tm, 1), F32)],
        compiler_params=_params(("arbitrary", "arbitrary")),
        name="ffn",
    )(*args)
    return res[0], res[1:]


def _pool_kernel(x_ref, xp_ref, xn_ref, g_ref, sh_ref, sc_ref, gt_ref, w_ref, ps_ref, o_ref,
                 hbuf, lev_a, lev_b, *, tp, seq_len):
    pos0 = (pl.program_id(0) * tp) % seq_len
    gmod = g_ref[...] * (1.0 + sc_ref[0])
    sh = sh_ref[0]
    x = x_ref[...]
    d = x.shape[-1]

    def norm(v):
        return (v * lax.rsqrt(jnp.mean(v * v, axis=-1, keepdims=True) + EPS)) * gmod + sh

    gc = d // len(POOL_WINDOWS)
    t0 = POOL_PRE
    span = tp + POOL_PRE

    hp = norm(xp_ref[...])
    hn = norm(xn_ref[...])
    hbuf[0:SUBLANES, :] = jnp.zeros((SUBLANES, d), F32)
    lev_a[0:SUBLANES, :] = jnp.zeros((SUBLANES, gc), F32)
    lev_b[0:SUBLANES, :] = jnp.zeros((SUBLANES, gc), F32)
    hbuf[SUBLANES:t0, :] = jnp.where(pos0 > 0, hp, 0.0)
    hbuf[t0:t0 + tp, :] = norm(x)
    hbuf[t0 + tp:, :] = jnp.where(pos0 + tp < seq_len, hn, 0.0)

    pos = pos0 + lax.broadcasted_iota(jnp.int32, (tp, 1), 0)
    outs = []
    for gi, w in enumerate(POOL_WINDOWS):
        cols = slice(gi * gc, (gi + 1) * gc)
        src, dst, other = hbuf, lev_a, lev_b
        src_cols = cols
        step = 1
        while step < w:
            cur = src[SUBLANES:SUBLANES + span, src_cols] + src[SUBLANES - step:SUBLANES - step + span, src_cols]
            step *= 2
            if step < w or w > 2:
                dst[SUBLANES:SUBLANES + span, :] = cur
                src, dst, other = dst, other, dst
                src_cols = slice(None)
        if w == 2:
            win = cur[t0 - SUBLANES:t0 - SUBLANES + tp, :]
        else:
            end = t0 + w // 2 - 1
            win = src[end:end + tp, :]
        lo = jnp.maximum(pos - w // 2, 0)
        hi = jnp.minimum(pos + w // 2, seq_len)
        cnt = (hi - lo).astype(F32)
        p = win / cnt - hbuf[t0:t0 + tp, cols]
        outs.append(jnp.dot(p.astype(BF16), w_ref[gi], preferred_element_type=F32))
    o_ref[...] = x + (gt_ref[0] * ps_ref[...]) * jnp.concatenate(outs, axis=-1)


def _pool(x, gain, mod, sub, pool_w, pool_scale, seq, tp=256):
    t, d = x.shape
    hb = tp // SUBLANES
    nhalo = t // SUBLANES
    gc = d // len(POOL_WINDOWS)
    kern = functools.partial(_pool_kernel, tp=tp, seq_len=seq[0])
    rows = POOL_PRE + tp + POOL_POST
    return pl.pallas_call(
        kern,
        grid=(t // tp,),
        in_specs=[
            pl.BlockSpec((tp, d), lambda i: (i, 0)),
            pl.BlockSpec((SUBLANES, d), lambda i: (jnp.maximum(i * hb - 1, 0), 0)),
            pl.BlockSpec((SUBLANES, d), lambda i: (jnp.minimum((i + 1) * hb, nhalo - 1), 0)),
            pl.BlockSpec((1, d), lambda i: (0, 0)),
            *_mod_specs(tp, seq, sub, d, 1),
            pl.BlockSpec(pool_w.shape, lambda i: (0, 0, 0)),
            pl.BlockSpec((1, d), lambda i: (0, 0)),
        ],
        out_specs=pl.BlockSpec((tp, d), lambda i: (i, 0)),
        out_shape=jax.ShapeDtypeStruct((t, d), F32),
        scratch_shapes=[pltpu.VMEM((rows, d), F32), pltpu.VMEM((rows, gc), F32), pltpu.VMEM((rows, gc), F32)],
        compiler_params=_params(("arbitrary",)),
        name="pool_mixer",
    )(x, x, x, gain.reshape(1, d), mod, mod, mod, pool_w, pool_scale.reshape(1, d))


def _rope_tables(s):
    t = jnp.arange(s)
    r = (t // GRID_W).astype(F32)
    c = (t % GRID_W).astype(F32)
    inv = ROPE_THETA ** (-jnp.arange(0, AXIS_DIM, 2, dtype=F32) / AXIS_DIM)
    ang_r = r[:, None] * inv[None, :]
    ang_c = c[:, None] * inv[None, :]
    cos = jnp.concatenate([jnp.cos(ang_r)] * 2 + [jnp.cos(ang_c)] * 2, axis=-1)
    sin = jnp.concatenate([-jnp.sin(ang_r), jnp.sin(ang_r), -jnp.sin(ang_c), jnp.sin(ang_c)], axis=-1)
    return cos, sin


def _head_norm_rope(xh, gain, cos, sin, first_half):
    ms = jnp.mean(xh * xh, axis=-1, keepdims=True)
    y = xh * lax.rsqrt(ms + EPS) * gain
    half = AXIS_DIM // 2
    partner = jnp.where(first_half, pltpu.roll(y, HEAD_DIM - half, 1), pltpu.roll(y, half, 1))
    return y * cos + partner * sin


def _qkv_kernel(x_ref, g_ref, sh_ref, sc_ref, w_ref, qg_ref, kg_ref, cos_ref, sin_ref,
                q_ref, k_ref, v_ref, *, n_heads, n_kv, n_split):
    qg = qg_ref[...] * Q_SCALE
    kg = kg_ref[...]
    k0 = n_heads * HEAD_DIM
    rs = x_ref.shape[0] // n_split
    lane = lax.broadcasted_iota(jnp.int32, (rs, HEAD_DIM), 1)
    first_half = (lane % AXIS_DIM) < (AXIS_DIM // 2)
    for r in range(n_split):
        rows = slice(r * rs, (r + 1) * rs)
        h = _mod_norm(x_ref[rows, :], g_ref[...], sh_ref[0], sc_ref[0]).astype(BF16)
        qkv = jnp.dot(h, w_ref[...], preferred_element_type=F32)
        cos = cos_ref[rows, :]
        sin = sin_ref[rows, :]
        for hd in range(n_heads):
            cols = slice(hd * HEAD_DIM, (hd + 1) * HEAD_DIM)
            q_ref[rows, cols] = _head_norm_rope(qkv[:, cols], qg, cos, sin, first_half).astype(BF16)
        for hd in range(n_kv):
            cols = slice(k0 + hd * HEAD_DIM, k0 + (hd + 1) * HEAD_DIM)
            k_ref[rows, hd * HEAD_DIM:(hd + 1) * HEAD_DIM] = _head_norm_rope(
                qkv[:, cols], kg, cos, sin, first_half).astype(BF16)
        v_ref[rows, :] = qkv[:, k0 + n_kv * HEAD_DIM:].astype(BF16)


def _qkv(x, gain, mod, sub, w_qkv, q_g, k_g, cos, sin, seq, tm=512, n_split=2):
    t, d = x.shape
    n_tot = w_qkv.shape[1] // HEAD_DIM
    n_heads = d // HEAD_DIM
    n_kv = (n_tot - n_heads) // 2
    seq_len = seq[0]
    pos_block = lambda i: (((i * tm) % seq_len) // tm, 0)
    kern = functools.partial(_qkv_kernel, n_heads=n_heads, n_kv=n_kv, n_split=n_split)
    return pl.pallas_call(
        kern,
        grid=(t // tm,),
        in_specs=[
            pl.BlockSpec((tm, d), lambda i: (i, 0)),
            pl.BlockSpec((1, d), lambda i: (0, 0)),
            *_mod_specs(tm, seq, sub, d, 1)[:2],
            pl.BlockSpec(w_qkv.shape, lambda i: (0, 0)),
            pl.BlockSpec((1, HEAD_DIM), lambda i: (0, 0)),
            pl.BlockSpec((1, HEAD_DIM), lambda i: (0, 0)),
            pl.BlockSpec((tm, HEAD_DIM), pos_block),
            pl.BlockSpec((tm, HEAD_DIM), pos_block),
        ],
        out_specs=[
            pl.BlockSpec((tm, n_heads * HEAD_DIM), lambda i: (i, 0)),
            pl.BlockSpec((tm, n_kv * HEAD_DIM), lambda i: (i, 0)),
            pl.BlockSpec((tm, n_kv * HEAD_DIM), lambda i: (i, 0)),
        ],
        out_shape=[
            jax.ShapeDtypeStruct((t, n_heads * HEAD_DIM), BF16),
            jax.ShapeDtypeStruct((t, n_kv * HEAD_DIM), BF16),
            jax.ShapeDtypeStruct((t, n_kv * HEAD_DIM), BF16),
        ],
        compiler_params=_params(("arbitrary",)),
        name="qkv_norm_rope",
    )(x, gain.reshape(1, d), mod, mod, w_qkv, q_g.reshape(1, HEAD_DIM), k_g.reshape(1, HEAD_DIM), cos, sin)


def _flash_kernel(q_ref, *refs, n_row_split):
    o_ref = refs[-1]
    chunks = []
    for c in range(len(refs) // 2):
        vs = refs[2 * c + 1][...]
        chunks.append((refs[2 * c][...], jnp.concatenate([vs, jnp.ones_like(vs)], axis=-1)))
    nblk = chunks[0][0].shape[0] // HEAD_DIM
    rs = q_ref.shape[0] // n_row_split
    for g in range(GQA_GROUP):
        cols = slice(g * HEAD_DIM, (g + 1) * HEAD_DIM)
        for r in range(n_row_split):
            rows = slice(r * rs, (r + 1) * rs)
            q = q_ref[rows, cols]
            m = acc = None
            for ks, v_ext in chunks:
                s = lax.dot_general(q, ks, (((1,), (1,)), ((), ())), preferred_element_type=F32)
                blocks = [s[:, c * HEAD_DIM:(c + 1) * HEAD_DIM] for c in range(nblk)]
                m_cur = jnp.max(functools.reduce(jnp.maximum, blocks), axis=-1, keepdims=True)
                m_new = jnp.broadcast_to(m_cur, blocks[0].shape) if m is None else jnp.maximum(m, m_cur)
                p = jnp.concatenate([jnp.exp2(blk - m_new) for blk in blocks], axis=-1).astype(BF16)
                pv = jnp.dot(p, v_ext, preferred_element_type=F32)
                if m is None:
                    acc = pv
                else:
                    alpha = jnp.exp2(m - m_new)
                    acc = jnp.concatenate([alpha, alpha], axis=-1) * acc + pv
                m = m_new
            o_ref[rows, cols] = (acc[:, :HEAD_DIM] / acc[:, HEAD_DIM:]).astype(o_ref.dtype)


def _flash(q, k, v, seq_len, tq=512, n_row_split=4):
    t, dq = q.shape
    n_kv = k.shape[1] // HEAD_DIM
    qw = GQA_GROUP * HEAD_DIM
    tk = min(seq_len, KV_CHUNK)
    n_chunks = seq_len // tk
    nq = seq_len // tq
    kv_specs, kv_args = [], []
    for c in range(n_chunks):
        spec = pl.BlockSpec((tk, HEAD_DIM), lambda b, h, i, c=c: (b * n_chunks + c, h))
        kv_specs += [spec, spec]
        kv_args += [k, v]
    kern = functools.partial(_flash_kernel, n_row_split=n_row_split)
    return pl.pallas_call(
        kern,
        grid=(t // seq_len, n_kv, nq),
        in_specs=[pl.BlockSpec((tq, qw), lambda b, h, i: (b * nq + i, h)), *kv_specs],
        out_specs=pl.BlockSpec((tq, qw), lambda b, h, i: (b * nq + i, h)),
        out_shape=jax.ShapeDtypeStruct((t, dq), BF16),
        compiler_params=_params(("arbitrary", "arbitrary", "arbitrary")),
        name="flash_attention",
    )(q, *kv_args)


def _oproj_kernel(x_ref, a_ref, gt_ref, w_ref, o_ref):
    m = jnp.dot(a_ref[...], w_ref[...], preferred_element_type=F32)
    o_ref[...] = x_ref[...] + gt_ref[0] * m


def _oproj(x, attn, mod, sub, w_o, seq, tm=512):
    t, d = x.shape
    return pl.pallas_call(
        _oproj_kernel,
        grid=(t // tm,),
        in_specs=[
            pl.BlockSpec((tm, d), lambda i: (i, 0)),
            pl.BlockSpec((tm, attn.shape[1]), lambda i: (i, 0)),
            _mod_specs(tm, seq, sub, d, 1)[2],
            pl.BlockSpec(w_o.shape, lambda i: (0, 0)),
        ],
        out_specs=pl.BlockSpec((tm, d), lambda i: (i, 0)),
        out_shape=jax.ShapeDtypeStruct((t, d), F32),
        compiler_params=_params(("arbitrary",)),
        name="attn_out_proj",
    )(x, attn, mod, w_o)


def kernel(x_prompt, x_sample, c_prompt, c_sample, ada_w, ada_b, norm_g, ffn_w_in, ffn_w_out,
           pool_w, pool_scale, attn_w_qkv, attn_q_g, attn_k_g, attn_w_o):
    bp, sp, d = x_prompt.shape
    bs, ss, _ = x_sample.shape
    depth = ada_w.shape[0]
    assert sp % GRID_W == 0 and ss % GRID_W == 0
    nseq = bp + bs
    c_all = jnp.concatenate([c_prompt, c_sample], axis=0)
    ada_b3 = ada_b.reshape(depth, 1, ada_b.shape[1])
    cos, sin = _rope_tables(max(sp, ss))

    xs = [x_prompt.reshape(bp * sp, d), x_sample.reshape(bs * ss, d)]
    seqs = [(sp, 0), (ss, bp)]
    n_groups = len(xs)
    d_ff = ffn_w_out.shape[2]
    nk = d_ff // FFN_TF
    tiles = [x.shape[0] // FFN_TM for x in xs]

    mods = {0: _ada_mod(c_all, ada_w, ada_b, 0, 1)[0]}
    if depth > 2:
        later = _ada_mod(c_all, ada_w, ada_b, 2, depth - 2)
        mods.update({2 + l: later[l] for l in range(depth - 2)})
    pool_wb = pool_w.astype(BF16)
    w_qkv = {j: attn_w_qkv[j].astype(BF16) for j in range(1, attn_w_qkv.shape[0])}
    w_o = {j: attn_w_o[j].astype(BF16) for j in range(1, attn_w_o.shape[0])}
    w_in_b = ffn_w_in[0, 0].astype(BF16).reshape(d, 2 * nk, FFN_TF).transpose(1, 0, 2)
    w_out_b = ffn_w_out[0, 0].astype(BF16)

    def ffn_stage(xs, li, fi, w_in_b, w_out_b, mod):
        nli, nfi = divmod(li * 2 + fi + 1, 2)
        jobs = [[] for _ in xs]
        if nli < depth:
            jobs[0].append(("w_in", _cast_blocked_job(ffn_w_in, (nli, nfi), tiles[0], nk, FFN_TF)))
            jobs[1].append(("w_out", _cast_job(ffn_w_out, (nli, nfi), tiles[1], nk, False)))
        if (li, fi) == (0, 0) and depth >= 2:
            jobs[0].append(("w_qkv", _cast_part_job(attn_w_qkv, (0,), tiles[0], nk, SIDE_STEPS)))
            jobs[1].append(("w_o", _cast_part_job(attn_w_o, (0,), tiles[1], nk, SIDE_STEPS)))
        if (li, fi) == (0, 1) and depth >= 2:
            for g in range(n_groups):
                jobs[g].append(("mod%d" % g, _ada_job(c_all, ada_w, ada_b3, 1, g, n_groups, tiles[g], nk)))
        sub = 0 if fi == 0 else 2
        outs, made = [], {}
        for x, seq, jl in zip(xs, seqs, jobs):
            y, extra = _ffn(x, norm_g[li, sub], mod, sub, w_in_b, w_out_b, seq, jobs=[j for _, j in jl], tm=FFN_TM)
            outs.append(y)
            made.update({name: e for (name, _), e in zip(jl, extra)})
        return outs, made

    for i in range(depth):
        mod = mods[i].reshape(nseq * N_MOD_ROWS, 1, d)
        j = i // 2
        xs, made = ffn_stage(xs, i, 0, w_in_b, w_out_b, mod)
        w_in_b, w_out_b = made.get("w_in"), made.get("w_out")
        if "w_qkv" in made:
            w_qkv[0], w_o[0] = made["w_qkv"], made["w_o"]
        if i % 2 == 0:
            xs = [_pool(x, norm_g[i, 1], mod, 1, pool_wb[j], pool_scale[j], seq) for x, seq in zip(xs, seqs)]
        else:
            nxt = []
            for x, seq in zip(xs, seqs):
                q, k, v = _qkv(x, norm_g[i, 1], mod, 1, w_qkv[j], attn_q_g[j], attn_k_g[j], cos, sin, seq)
                a = _flash(q, k, v, seq[0])
                nxt.append(_oproj(x, a, mod, 1, w_o[j], seq))
            xs = nxt
        xs, made = ffn_stage(xs, i, 1, w_in_b, w_out_b, mod)
        w_in_b, w_out_b = made.get("w_in"), made.get("w_out")
        if "mod0" in made:
            mods[1] = jnp.concatenate([made["mod%d" % g] for g in range(n_groups)], axis=1)

    return (xs[0].reshape(bp, sp, d), xs[1].reshape(bs, ss, d))
```

```python
import functools
import math

import jax
import jax.numpy as jnp
from jax import lax
from jax.experimental import pallas as pl
from jax.experimental.pallas import tpu as pltpu

F32 = jnp.float32
BF16 = jnp.bfloat16

EPS = 1e-6
HEAD_DIM = 128
AXIS_DIM = HEAD_DIM // 2
ROPE_THETA = 10000.0
GRID_W = 64
GQA_GROUP = 4
POOL_WINDOWS = (2, 4, 8, 16)
SUBLANES = 8
POOL_PRE = 2 * SUBLANES
POOL_POST = SUBLANES
Q_SCALE = math.log2(math.e) / math.sqrt(HEAD_DIM)
N_MOD_ROWS = 9
NORM_ROWS = 256
KV_CHUNK = 2048
FFN_TM = 1024
FFN_TF = 512
SIDE_STEPS = 8

VMEM_LIMIT_BYTES = 60 * 1024 * 1024


def _params(semantics):
    return pltpu.CompilerParams(dimension_semantics=semantics, vmem_limit_bytes=VMEM_LIMIT_BYTES)


def _mod_norm(x, g, shift, scl):
    ms = jnp.mean(x * x, axis=-1, keepdims=True)
    y = x * lax.rsqrt(ms + EPS)
    return (y * g) * (1.0 + scl) + shift


def _ada_kernel(c_ref, w_ref, b_ref, o_ref):
    c = c_ref[...]
    s = (c * jax.nn.sigmoid(c)).astype(BF16)
    w = w_ref[0].astype(BF16)
    o_ref[0] = jnp.dot(s, w, preferred_element_type=F32) + b_ref[0]


def _ada_mod(c_all, ada_w, ada_b, first, count, tn=1024):
    depth, d, n = ada_w.shape
    nseq = c_all.shape[0]
    return pl.pallas_call(
        _ada_kernel,
        grid=(count, n // tn),
        in_specs=[
            pl.BlockSpec((nseq, d), lambda l, j: (0, 0)),
            pl.BlockSpec((1, d, tn), lambda l, j: (first + l, 0, j)),
            pl.BlockSpec((1, 1, tn), lambda l, j: (first + l, 0, j)),
        ],
        out_specs=pl.BlockSpec((1, nseq, tn), lambda l, j: (l, 0, j)),
        out_shape=jax.ShapeDtypeStruct((count, nseq, n), F32),
        compiler_params=_params(("arbitrary", "arbitrary")),
        name="ada_mod",
    )(c_all, ada_w, ada_b.reshape(depth, 1, n))


def _mod_specs(tm, seq, sub, d, ngrid):
    seq_len, seq0 = seq
    specs = []
    for r in range(3):
        row = sub * 3 + r
        if ngrid == 1:
            imap = lambda i, row=row: ((seq0 + (i * tm) // seq_len) * N_MOD_ROWS + row, 0, 0)
        else:
            imap = lambda i, k, row=row: ((seq0 + (i * tm) // seq_len) * N_MOD_ROWS + row, 0, 0)
        specs.append(pl.BlockSpec((1, 1, d), imap))
    return specs


def _cast_job(src, prefix, ni, nk, by_rows):
    rows, cols = src.shape[-2:]
    nr, nc = (ni, nk) if by_rows else (nk, ni)
    assert rows % nr == 0 and cols % nc == 0
    blk = (rows // nr, cols // nc)
    pick = (lambda i, k: (i, k)) if by_rows else (lambda i, k: (k, i))
    lead = (None,) * len(prefix)

    def body(ins, out):
        out[...] = ins[0][...].astype(BF16)

    return ([pl.BlockSpec(lead + blk, lambda i, k: tuple(prefix) + pick(i, k))], [src],
            pl.BlockSpec(blk, pick), jax.ShapeDtypeStruct((rows, cols), BF16), body)


def _cast_part_job(src, prefix, ni, nk, kc):
    rows, cols = src.shape[-2:]
    assert kc <= nk and rows % ni == 0 and cols % kc == 0 and (cols // kc) % 128 == 0
    blk = (rows // ni, cols // kc)
    lead = (None,) * len(prefix)
    pick = lambda i, k: (i, jnp.minimum(k, kc - 1))

    def body(ins, out):
        out[...] = ins[0][...].astype(BF16)

    return ([pl.BlockSpec(lead + blk, lambda i, k: tuple(prefix) + pick(i, k))], [src],
            pl.BlockSpec(blk, pick), jax.ShapeDtypeStruct((rows, cols), BF16), body)


def _ffn_kernel(x_ref, g_ref, sh_ref, sc_ref, gt_ref, wg_ref, wu_ref, wo_ref, *rest, n_split, sides):
    n_side_in = sum(n for n, _ in sides)
    side_ins, o_ref = rest[:n_side_in], rest[n_side_in]
    side_outs = rest[n_side_in + 1:n_side_in + 1 + len(sides)]
    h_ref, inv_ref = rest[n_side_in + 1 + len(sides):]
    k = pl.program_id(1)
    last = pl.num_programs(1) - 1

    def norm_rows(r0, n):
        gmod = g_ref[...] * (1.0 + sc_ref[0])
        shift = sh_ref[0]
        for c0 in range(r0, r0 + n, NORM_ROWS):
            rows = slice(c0, c0 + NORM_ROWS)
            x = x_ref[rows, :]
            inv_ref[rows, :] = lax.rsqrt(jnp.mean(x * x, axis=-1, keepdims=True) + EPS)
        for c0 in range(r0, r0 + n, NORM_ROWS):
            rows = slice(c0, c0 + NORM_ROWS)
            h_ref[rows, :] = ((x_ref[rows, :] * inv_ref[rows, :]) * gmod + shift).astype(BF16)

    def step(mode):
        at = 0
        for (n, body), out in zip(sides, side_outs):
            body(side_ins[at:at + n], out)
            at += n
        rs = h_ref.shape[0] // n_split
        tn = wg_ref.shape[1]
        for r in range(n_split):
            rows = slice(r * rs, (r + 1) * rs)
            if mode == "first":
                norm_rows(r * rs, rs)
            hb = h_ref[rows, :]
            g = jnp.dot(hb, wg_ref[...], preferred_element_type=F32)
            u = jnp.dot(hb, wu_ref[...], preferred_element_type=F32)
            a = (g * jax.nn.sigmoid(g) * u).astype(BF16)
            for c in range(o_ref.shape[1] // tn):
                cols = slice(c * tn, (c + 1) * tn)
                part = jnp.dot(a, wo_ref[:, cols], preferred_element_type=F32)
                if mode == "first":
                    o_ref[rows, cols] = part
                elif mode == "middle":
                    o_ref[rows, cols] += part
                else:
                    o_ref[rows, cols] = x_ref[rows, cols] + (0.5 * gt_ref[0][:, cols]) * (o_ref[rows, cols] + part)

    @pl.when(k == 0)
    def _():
        step("first")

    @pl.when(jnp.logical_and(k > 0, k < last))
    def _():
        step("middle")

    @pl.when(k == last)
    def _():
        step("last")


def _ffn(x, gain, mod, sub, w_in, w_out, seq, jobs=(), tm=FFN_TM, tf=FFN_TF, n_split=2):
    t, d = x.shape
    d_ff = w_out.shape[0]
    nk = d_ff // tf
    ni = t // tm
    assert nk >= 2
    in_specs = [
        pl.BlockSpec((tm, d), lambda i, k: (i, 0)),
        pl.BlockSpec((1, d), lambda i, k: (0, 0)),
        *_mod_specs(tm, seq, sub, d, 2),
        pl.BlockSpec((d, tf), lambda i, k: (0, k)),
        pl.BlockSpec((d, tf), lambda i, k: (0, k + nk)),
        pl.BlockSpec((tf, d), lambda i, k: (k, 0)),
    ]
    args = [x, gain.reshape(1, d), mod, mod, mod, w_in, w_in, w_out]
    out_specs = [pl.BlockSpec((tm, d), lambda i, k: (i, 0))]
    out_shape = [jax.ShapeDtypeStruct((t, d), F32)]
    sides = []
    for j_in_specs, j_args, j_out_spec, j_out_shape, body in jobs:
        in_specs += j_in_specs
        args += j_args
        out_specs.append(j_out_spec)
        out_shape.append(j_out_shape)
        sides.append((len(j_args), body))
    res = pl.pallas_call(
        functools.partial(_ffn_kernel, n_split=n_split, sides=tuple(sides)),
        grid=(ni, nk),
        in_specs=in_specs,
        out_specs=out_specs,
        out_shape=out_shape,
        scratch_shapes=[pltpu.VMEM((tm, d), BF16), pltpu.VMEM((tm, 1), F32)],
        compiler_params=_params(("arbitrary", "arbitrary")),
        name="ffn",
    )(*args)
    return res[0], res[1:]


def _pool_kernel(x_ref, xp_ref, xn_ref, g_ref, sh_ref, sc_ref, gt_ref, w_ref, ps_ref, o_ref,
                 hbuf, lev_a, lev_b, *, tp, seq_len):
    pos0 = (pl.program_id(0) * tp) % seq_len
    gmod = g_ref[...] * (1.0 + sc_ref[0])
    sh = sh_ref[0]
    x = x_ref[...]
    d = x.shape[-1]

    def norm(v):
        return (v * lax.rsqrt(jnp.mean(v * v, axis=-1, keepdims=True) + EPS)) * gmod + sh

    gc = d // len(POOL_WINDOWS)
    t0 = POOL_PRE
    span = tp + POOL_PRE

    hp = norm(xp_ref[...])
    hn = norm(xn_ref[...])
    hbuf[0:SUBLANES, :] = jnp.zeros((SUBLANES, d), F32)
    lev_a[0:SUBLANES, :] = jnp.zeros((SUBLANES, gc), F32)
    lev_b[0:SUBLANES, :] = jnp.zeros((SUBLANES, gc), F32)
    hbuf[SUBLANES:t0, :] = jnp.where(pos0 > 0, hp, 0.0)
    hbuf[t0:t0 + tp, :] = norm(x)
    hbuf[t0 + tp:, :] = jnp.where(pos0 + tp < seq_len, hn, 0.0)

    pos = pos0 + lax.broadcasted_iota(jnp.int32, (tp, 1), 0)
    outs = []
    for gi, w in enumerate(POOL_WINDOWS):
        cols = slice(gi * gc, (gi + 1) * gc)
        src, dst, other = hbuf, lev_a, lev_b
        src_cols = cols
        step = 1
        while step < w:
            cur = src[SUBLANES:SUBLANES + span, src_cols] + src[SUBLANES - step:SUBLANES - step + span, src_cols]
            step *= 2
            if step < w or w > 2:
                dst[SUBLANES:SUBLANES + span, :] = cur
                src, dst, other = dst, other, dst
                src_cols = slice(None)
        if w == 2:
            win = cur[t0 - SUBLANES:t0 - SUBLANES + tp, :]
        else:
            end = t0 + w // 2 - 1
            win = src[end:end + tp, :]
        lo = jnp.maximum(pos - w // 2, 0)
        hi = jnp.minimum(pos + w // 2, seq_len)
        cnt = (hi - lo).astype(F32)
        p = win / cnt - hbuf[t0:t0 + tp, cols]
        outs.append(jnp.dot(p.astype(BF16), w_ref[gi], preferred_element_type=F32))
    o_ref[...] = x + (gt_ref[0] * ps_ref[...]) * jnp.concatenate(outs, axis=-1)


def _pool(x, gain, mod, sub, pool_w, pool_scale, seq, tp=256):
    t, d = x.shape
    hb = tp // SUBLANES
    nhalo = t // SUBLANES
    gc = d // len(POOL_WINDOWS)
    kern = functools.partial(_pool_kernel, tp=tp, seq_len=seq[0])
    rows = POOL_PRE + tp + POOL_POST
    return pl.pallas_call(
        kern,
        grid=(t // tp,),
        in_specs=[
            pl.BlockSpec((tp, d), lambda i: (i, 0)),
            pl.BlockSpec((SUBLANES, d), lambda i: (jnp.maximum(i * hb - 1, 0), 0)),
            pl.BlockSpec((SUBLANES, d), lambda i: (jnp.minimum((i + 1) * hb, nhalo - 1), 0)),
            pl.BlockSpec((1, d), lambda i: (0, 0)),
            *_mod_specs(tp, seq, sub, d, 1),
            pl.BlockSpec(pool_w.shape, lambda i: (0, 0, 0)),
            pl.BlockSpec((1, d), lambda i: (0, 0)),
        ],
        out_specs=pl.BlockSpec((tp, d), lambda i: (i, 0)),
        out_shape=jax.ShapeDtypeStruct((t, d), F32),
        scratch_shapes=[pltpu.VMEM((rows, d), F32), pltpu.VMEM((rows, gc), F32), pltpu.VMEM((rows, gc), F32)],
        compiler_params=_params(("arbitrary",)),
        name="pool_mixer",
    )(x, x, x, gain.reshape(1, d), mod, mod, mod, pool_w, pool_scale.reshape(1, d))


def _rope_tables(s):
    t = jnp.arange(s)
    r = (t // GRID_W).astype(F32)
    c = (t % GRID_W).astype(F32)
    inv = ROPE_THETA ** (-jnp.arange(0, AXIS_DIM, 2, dtype=F32) / AXIS_DIM)
    ang_r = r[:, None] * inv[None, :]
    ang_c = c[:, None] * inv[None, :]
    cos = jnp.concatenate([jnp.cos(ang_r)] * 2 + [jnp.cos(ang_c)] * 2, axis=-1)
    sin = jnp.concatenate([-jnp.sin(ang_r), jnp.sin(ang_r), -jnp.sin(ang_c), jnp.sin(ang_c)], axis=-1)
    return cos, sin


def _head_norm_rope(xh, gain, cos, sin, first_half):
    ms = jnp.mean(xh * xh, axis=-1, keepdims=True)
    y = xh * lax.rsqrt(ms + EPS) * gain
    half = AXIS_DIM // 2
    partner = jnp.where(first_half, pltpu.roll(y, HEAD_DIM - half, 1), pltpu.roll(y, half, 1))
    return y * cos + partner * sin


def _qkv_kernel(x_ref, g_ref, sh_ref, sc_ref, w_ref, qg_ref, kg_ref, cos_ref, sin_ref,
                q_ref, k_ref, v_ref, *, n_heads, n_kv, n_split):
    qg = qg_ref[...] * Q_SCALE
    kg = kg_ref[...]
    k0 = n_heads * HEAD_DIM
    rs = x_ref.shape[0] // n_split
    lane = lax.broadcasted_iota(jnp.int32, (rs, HEAD_DIM), 1)
    first_half = (lane % AXIS_DIM) < (AXIS_DIM // 2)
    for r in range(n_split):
        rows = slice(r * rs, (r + 1) * rs)
        h = _mod_norm(x_ref[rows, :], g_ref[...], sh_ref[0], sc_ref[0]).astype(BF16)
        qkv = jnp.dot(h, w_ref[...], preferred_element_type=F32)
        cos = cos_ref[rows, :]
        sin = sin_ref[rows, :]
        for hd in range(n_heads):
            cols = slice(hd * HEAD_DIM, (hd + 1) * HEAD_DIM)
            q_ref[rows, cols] = _head_norm_rope(qkv[:, cols], qg, cos, sin, first_half).astype(BF16)
        for hd in range(n_kv):
            cols = slice(k0 + hd * HEAD_DIM, k0 + (hd + 1) * HEAD_DIM)
            k_ref[rows, hd * HEAD_DIM:(hd + 1) * HEAD_DIM] = _head_norm_rope(
                qkv[:, cols], kg, cos, sin, first_half).astype(BF16)
        v_ref[rows, :] = qkv[:, k0 + n_kv * HEAD_DIM:].astype(BF16)


def _qkv(x, gain, mod, sub, w_qkv, q_g, k_g, cos, sin, seq, tm=512, n_split=2):
    t, d = x.shape
    n_tot = w_qkv.shape[1] // HEAD_DIM
    n_heads = d // HEAD_DIM
    n_kv = (n_tot - n_heads) // 2
    seq_len = seq[0]
    pos_block = lambda i: (((i * tm) % seq_len) // tm, 0)
    kern = functools.partial(_qkv_kernel, n_heads=n_heads, n_kv=n_kv, n_split=n_split)
    return pl.pallas_call(
        kern,
        grid=(t // tm,),
        in_specs=[
            pl.BlockSpec((tm, d), lambda i: (i, 0)),
            pl.BlockSpec((1, d), lambda i: (0, 0)),
            *_mod_specs(tm, seq, sub, d, 1)[:2],
            pl.BlockSpec(w_qkv.shape, lambda i: (0, 0)),
            pl.BlockSpec((1, HEAD_DIM), lambda i: (0, 0)),
            pl.BlockSpec((1, HEAD_DIM), lambda i: (0, 0)),
            pl.BlockSpec((tm, HEAD_DIM), pos_block),
            pl.BlockSpec((tm, HEAD_DIM), pos_block),
        ],
        out_specs=[
            pl.BlockSpec((tm, n_heads * HEAD_DIM), lambda i: (i, 0)),
            pl.BlockSpec((tm, n_kv * HEAD_DIM), lambda i: (i, 0)),
            pl.BlockSpec((tm, n_kv * HEAD_DIM), lambda i: (i, 0)),
        ],
        out_shape=[
            jax.ShapeDtypeStruct((t, n_heads * HEAD_DIM), BF16),
            jax.ShapeDtypeStruct((t, n_kv * HEAD_DIM), BF16),
            jax.ShapeDtypeStruct((t, n_kv * HEAD_DIM), BF16),
        ],
        compiler_params=_params(("arbitrary",)),
        name="qkv_norm_rope",
    )(x, gain.reshape(1, d), mod, mod, w_qkv, q_g.reshape(1, HEAD_DIM), k_g.reshape(1, HEAD_DIM), cos, sin)


def _flash_kernel(q_ref, *refs, n_row_split):
    o_ref = refs[-1]
    chunks = []
    for c in range(len(refs) // 2):
        vs = refs[2 * c + 1][...]
        chunks.append((refs[2 * c][...], jnp.concatenate([vs, jnp.ones_like(vs)], axis=-1)))
    nblk = chunks[0][0].shape[0] // HEAD_DIM
    rs = q_ref.shape[0] // n_row_split
    for g in range(GQA_GROUP):
        cols = slice(g * HEAD_DIM, (g + 1) * HEAD_DIM)
        for r in range(n_row_split):
            rows = slice(r * rs, (r + 1) * rs)
            q = q_ref[rows, cols]
            m = acc = None
            for ks, v_ext in chunks:
                s = lax.dot_general(q, ks, (((1,), (1,)), ((), ())), preferred_element_type=F32)
                blocks = [s[:, c * HEAD_DIM:(c + 1) * HEAD_DIM] for c in range(nblk)]
                m_cur = jnp.max(functools.reduce(jnp.maximum, blocks), axis=-1, keepdims=True)
                m_new = jnp.broadcast_to(m_cur, blocks[0].shape) if m is None else jnp.maximum(m, m_cur)
                p = jnp.concatenate([jnp.exp2(blk - m_new) for blk in blocks], axis=-1).astype(BF16)
                pv = jnp.dot(p, v_ext, preferred_element_type=F32)
                if m is None:
                    acc = pv
                else:
                    alpha = jnp.exp2(m - m_new)
                    acc = jnp.concatenate([alpha, alpha], axis=-1) * acc + pv
                m = m_new
            o_ref[rows, cols] = (acc[:, :HEAD_DIM] / acc[:, HEAD_DIM:]).astype(o_ref.dtype)


def _flash(q, k, v, seq_len, tq=1024, n_row_split=8):
    t, dq = q.shape
    n_kv = k.shape[1] // HEAD_DIM
    qw = GQA_GROUP * HEAD_DIM
    tk = min(seq_len, KV_CHUNK)
    n_chunks = seq_len // tk
    nq = seq_len // tq
    kv_specs, kv_args = [], []
    for c in range(n_chunks):
        spec = pl.BlockSpec((tk, HEAD_DIM), lambda b, h, i, c=c: (b * n_chunks + c, h))
        kv_specs += [spec, spec]
        kv_args += [k, v]
    kern = functools.partial(_flash_kernel, n_row_split=n_row_split)
    return pl.pallas_call(
        kern,
        grid=(t // seq_len, n_kv, nq),
        in_specs=[pl.BlockSpec((tq, qw), lambda b, h, i: (b * nq + i, h)), *kv_specs],
        out_specs=pl.BlockSpec((tq, qw), lambda b, h, i: (b * nq + i, h)),
        out_shape=jax.ShapeDtypeStruct((t, dq), BF16),
        compiler_params=_params(("arbitrary", "arbitrary", "arbitrary")),
        name="flash_attention",
    )(q, *kv_args)


def _oproj_kernel(x_ref, a_ref, gt_ref, w_ref, o_ref):
    m = jnp.dot(a_ref[...], w_ref[...], preferred_element_type=F32)
    o_ref[...] = x_ref[...] + gt_ref[0] * m


def _oproj(x, attn, mod, sub, w_o, seq, tm=512):
    t, d = x.shape
    return pl.pallas_call(
        _oproj_kernel,
        grid=(t // tm,),
        in_specs=[
            pl.BlockSpec((tm, d), lambda i: (i, 0)),
            pl.BlockSpec((tm, attn.shape[1]), lambda i: (i, 0)),
            _mod_specs(tm, seq, sub, d, 1)[2],
            pl.BlockSpec(w_o.shape, lambda i: (0, 0)),
        ],
        out_specs=pl.BlockSpec((tm, d), lambda i: (i, 0)),
        out_shape=jax.ShapeDtypeStruct((t, d), F32),
        compiler_params=_params(("arbitrary",)),
        name="attn_out_proj",
    )(x, attn, mod, w_o)


def kernel(x_prompt, x_sample, c_prompt, c_sample, ada_w, ada_b, norm_g, ffn_w_in, ffn_w_out,
           pool_w, pool_scale, attn_w_qkv, attn_q_g, attn_k_g, attn_w_o):
    bp, sp, d = x_prompt.shape
    bs, ss, _ = x_sample.shape
    depth = ada_w.shape[0]
    assert sp % GRID_W == 0 and ss % GRID_W == 0
    nseq = bp + bs
    c_all = jnp.concatenate([c_prompt, c_sample], axis=0)
    cos, sin = _rope_tables(max(sp, ss))

    xs = [x_prompt.reshape(bp * sp, d), x_sample.reshape(bs * ss, d)]
    seqs = [(sp, 0), (ss, bp)]
    d_ff = ffn_w_out.shape[2]
    nk = d_ff // FFN_TF
    tiles = [x.shape[0] // FFN_TM for x in xs]

    mods = _ada_mod(c_all, ada_w, ada_b, 0, depth)
    pool_wb = pool_w.astype(BF16)
    w_qkv = {j: attn_w_qkv[j].astype(BF16) for j in range(1, attn_w_qkv.shape[0])}
    w_o = {j: attn_w_o[j].astype(BF16) for j in range(1, attn_w_o.shape[0])}
    w_in_b = ffn_w_in[0, 0].astype(BF16)
    w_out_b = ffn_w_out[0, 0].astype(BF16)

    def ffn_stage(xs, li, fi, w_in_b, w_out_b, mod):
        nli, nfi = divmod(li * 2 + fi + 1, 2)
        jobs = [[] for _ in xs]
        if nli < depth:
            jobs[0].append(("w_in", _cast_job(ffn_w_in, (nli, nfi), tiles[0], nk, True)))
            jobs[1].append(("w_out", _cast_job(ffn_w_out, (nli, nfi), tiles[1], nk, False)))
        if (li, fi) == (0, 0) and depth >= 2:
            jobs[0].append(("w_qkv", _cast_part_job(attn_w_qkv, (0,), tiles[0], nk, SIDE_STEPS)))
            jobs[1].append(("w_o", _cast_part_job(attn_w_o, (0,), tiles[1], nk, SIDE_STEPS)))
        sub = 0 if fi == 0 else 2
        outs, made = [], {}
        for x, seq, jl in zip(xs, seqs, jobs):
            y, extra = _ffn(x, norm_g[li, sub], mod, sub, w_in_b, w_out_b, seq, jobs=[j for _, j in jl])
            outs.append(y)
            made.update({name: e for (name, _), e in zip(jl, extra)})
        return outs, made

    for i in range(depth):
        mod = mods[i].reshape(nseq * N_MOD_ROWS, 1, d)
        j = i // 2
        xs, made = ffn_stage(xs, i, 0, w_in_b, w_out_b, mod)
        w_in_b, w_out_b = made.get("w_in"), made.get("w_out")
        if "w_qkv" in made:
            w_qkv[0], w_o[0] = made["w_qkv"], made["w_o"]
        if i % 2 == 0:
            xs = [_pool(x, norm_g[i, 1], mod, 1, pool_wb[j], pool_scale[j], seq) for x, seq in zip(xs, seqs)]
        else:
            nxt = []
            for x, seq in zip(xs, seqs):
                q, k, v = _qkv(x, norm_g[i, 1], mod, 1, w_qkv[j], attn_q_g[j], attn_k_g[j], cos, sin, seq)
                a = _flash(q, k, v, seq[0])
                nxt.append(_oproj(x, a, mod, 1, w_o[j], seq))
            xs = nxt
        xs, made = ffn_stage(xs, i, 1, w_in_b, w_out_b, mod)
        w_in_b, w_out_b = made.get("w_in"), made.get("w_out")

    return (xs[0].reshape(bp, sp, d), xs[1].reshape(bs, ss, d))
```

```python
import functools
import math

import jax
import jax.numpy as jnp
from jax import lax
from jax.experimental import pallas as pl
from jax.experimental.pallas import tpu as pltpu

F32 = jnp.float32
BF16 = jnp.bfloat16

EPS = 1e-6
HEAD_DIM = 128
AXIS_DIM = HEAD_DIM // 2
ROPE_THETA = 10000.0
GRID_W = 64
GQA_GROUP = 4
POOL_WINDOWS = (2, 4, 8, 16)
SUBLANES = 8
POOL_PRE = 2 * SUBLANES
POOL_POST = SUBLANES
Q_SCALE = math.log2(math.e) / math.sqrt(HEAD_DIM)
N_MOD_ROWS = 9
NORM_ROWS = 256
KV_CHUNK = 2048
FFN_TM = 1024
FFN_TF = 512
SIDE_STEPS = 8

VMEM_LIMIT_BYTES = 60 * 1024 * 1024


def _params(semantics):
    return pltpu.CompilerParams(dimension_semantics=semantics, vmem_limit_bytes=VMEM_LIMIT_BYTES)


def _mod_norm(x, g, shift, scl):
    ms = jnp.mean(x * x, axis=-1, keepdims=True)
    y = x * lax.rsqrt(ms + EPS)
    return (y * g) * (1.0 + scl) + shift


def _ada_kernel(c_ref, w_ref, b_ref, o_ref):
    c = c_ref[...]
    s = (c * jax.nn.sigmoid(c)).astype(BF16)
    w = w_ref[0].astype(BF16)
    o_ref[0] = jnp.dot(s, w, preferred_element_type=F32) + b_ref[0]


def _ada_mod(c_all, ada_w, ada_b, first, count, tn=1024):
    depth, d, n = ada_w.shape
    nseq = c_all.shape[0]
    return pl.pallas_call(
        _ada_kernel,
        grid=(count, n // tn),
        in_specs=[
            pl.BlockSpec((nseq, d), lambda l, j: (0, 0)),
            pl.BlockSpec((1, d, tn), lambda l, j: (first + l, 0, j)),
            pl.BlockSpec((1, 1, tn), lambda l, j: (first + l, 0, j)),
        ],
        out_specs=pl.BlockSpec((1, nseq, tn), lambda l, j: (l, 0, j)),
        out_shape=jax.ShapeDtypeStruct((count, nseq, n), F32),
        compiler_params=_params(("arbitrary", "arbitrary")),
        name="ada_mod",
    )(c_all, ada_w, ada_b.reshape(depth, 1, n))


def _mod_specs(tm, seq, sub, d, ngrid):
    seq_len, seq0 = seq
    specs = []
    for r in range(3):
        row = sub * 3 + r
        if ngrid == 1:
            imap = lambda i, row=row: ((seq0 + (i * tm) // seq_len) * N_MOD_ROWS + row, 0, 0)
        else:
            imap = lambda i, k, row=row: ((seq0 + (i * tm) // seq_len) * N_MOD_ROWS + row, 0, 0)
        specs.append(pl.BlockSpec((1, 1, d), imap))
    return specs


def _cast_job(src, prefix, ni, nk, by_rows):
    rows, cols = src.shape[-2:]
    nr, nc = (ni, nk) if by_rows else (nk, ni)
    assert rows % nr == 0 and cols % nc == 0
    blk = (rows // nr, cols // nc)
    pick = (lambda i, k: (i, k)) if by_rows else (lambda i, k: (k, i))
    lead = (None,) * len(prefix)

    def body(ins, out):
        out[...] = ins[0][...].astype(BF16)

    return ([pl.BlockSpec(lead + blk, lambda i, k: tuple(prefix) + pick(i, k))], [src],
            pl.BlockSpec(blk, pick), jax.ShapeDtypeStruct((rows, cols), BF16), body)


def _cast_part_job(src, prefix, ni, nk, kc):
    rows, cols = src.shape[-2:]
    assert kc <= nk and rows % ni == 0 and cols % kc == 0 and (cols // kc) % 128 == 0
    blk = (rows // ni, cols // kc)
    lead = (None,) * len(prefix)
    pick = lambda i, k: (i, jnp.minimum(k, kc - 1))

    def body(ins, out):
        out[...] = ins[0][...].astype(BF16)

    return ([pl.BlockSpec(lead + blk, lambda i, k: tuple(prefix) + pick(i, k))], [src],
            pl.BlockSpec(blk, pick), jax.ShapeDtypeStruct((rows, cols), BF16), body)


def _ffn_kernel(x_ref, g_ref, sh_ref, sc_ref, gt_ref, wg_ref, wu_ref, wo_ref, *rest, n_split, sides):
    n_side_in = sum(n for n, _ in sides)
    side_ins, o_ref = rest[:n_side_in], rest[n_side_in]
    side_outs = rest[n_side_in + 1:n_side_in + 1 + len(sides)]
    h_ref, inv_ref = rest[n_side_in + 1 + len(sides):]
    k = pl.program_id(1)
    last = pl.num_programs(1) - 1
    at = 0
    for (n, body), out in zip(sides, side_outs):
        body(side_ins[at:at + n], out)
        at += n

    def norm_rows(r0, n):
        gmod = g_ref[...] * (1.0 + sc_ref[0])
        shift = sh_ref[0]
        for c0 in range(r0, r0 + n, NORM_ROWS):
            rows = slice(c0, c0 + NORM_ROWS)
            x = x_ref[rows, :]
            inv_ref[rows, :] = lax.rsqrt(jnp.mean(x * x, axis=-1, keepdims=True) + EPS)
        for c0 in range(r0, r0 + n, NORM_ROWS):
            rows = slice(c0, c0 + NORM_ROWS)
            h_ref[rows, :] = ((x_ref[rows, :] * inv_ref[rows, :]) * gmod + shift).astype(BF16)

    def step(mode):
        rs = h_ref.shape[0] // n_split
        tn = wg_ref.shape[1]
        for r in range(n_split):
            rows = slice(r * rs, (r + 1) * rs)
            if mode == "first":
                norm_rows(r * rs, rs)
            hb = h_ref[rows, :]
            g = jnp.dot(hb, wg_ref[...], preferred_element_type=F32)
            u = jnp.dot(hb, wu_ref[...], preferred_element_type=F32)
            a = (g * jax.nn.sigmoid(g) * u).astype(BF16)
            for c in range(o_ref.shape[1] // tn):
                cols = slice(c * tn, (c + 1) * tn)
                part = jnp.dot(a, wo_ref[:, cols], preferred_element_type=F32)
                if mode == "first":
                    o_ref[rows, cols] = part
                elif mode == "middle":
                    o_ref[rows, cols] += part
                else:
                    o_ref[rows, cols] = x_ref[rows, cols] + (0.5 * gt_ref[0][:, cols]) * (o_ref[rows, cols] + part)

    @pl.when(k == 0)
    def _():
        step("first")

    @pl.when(jnp.logical_and(k > 0, k < last))
    def _():
        step("middle")

    @pl.when(k == last)
    def _():
        step("last")


def _ffn(x, gain, mod, sub, w_in, w_out, seq, jobs=(), tm=FFN_TM, tf=FFN_TF, n_split=2):
    t, d = x.shape
    d_ff = w_out.shape[0]
    nk = d_ff // tf
    ni = t // tm
    assert nk >= 2
    in_specs = [
        pl.BlockSpec((tm, d), lambda i, k: (i, 0)),
        pl.BlockSpec((1, d), lambda i, k: (0, 0)),
        *_mod_specs(tm, seq, sub, d, 2),
        pl.BlockSpec((d, tf), lambda i, k: (0, k)),
        pl.BlockSpec((d, tf), lambda i, k: (0, k + nk)),
        pl.BlockSpec((tf, d), lambda i, k: (k, 0)),
    ]
    args = [x, gain.reshape(1, d), mod, mod, mod, w_in, w_in, w_out]
    out_specs = [pl.BlockSpec((tm, d), lambda i, k: (i, 0))]
    out_shape = [jax.ShapeDtypeStruct((t, d), F32)]
    sides = []
    for j_in_specs, j_args, j_out_spec, j_out_shape, body in jobs:
        in_specs += j_in_specs
        args += j_args
        out_specs.append(j_out_spec)
        out_shape.append(j_out_shape)
        sides.append((len(j_args), body))
    res = pl.pallas_call(
        functools.partial(_ffn_kernel, n_split=n_split, sides=tuple(sides)),
        grid=(ni, nk),
        in_specs=in_specs,
        out_specs=out_specs,
        out_shape=out_shape,
        scratch_shapes=[pltpu.VMEM((tm, d), BF16), pltpu.VMEM((tm, 1), F32)],
        compiler_params=_params(("arbitrary", "arbitrary")),
        name="ffn",
    )(*args)
    return res[0], res[1:]


def _pool_kernel(x_ref, xp_ref, xn_ref, g_ref, sh_ref, sc_ref, gt_ref, w_ref, ps_ref, o_ref,
                 hbuf, lev_a, lev_b, *, tp, seq_len):
    pos0 = (pl.program_id(0) * tp) % seq_len
    gmod = g_ref[...] * (1.0 + sc_ref[0])
    sh = sh_ref[0]
    x = x_ref[...]
    d = x.shape[-1]

    def norm(v):
        return (v * lax.rsqrt(jnp.mean(v * v, axis=-1, keepdims=True) + EPS)) * gmod + sh

    gc = d // len(POOL_WINDOWS)
    t0 = POOL_PRE
    span = tp + POOL_PRE

    hp = norm(xp_ref[...])
    hn = norm(xn_ref[...])
    hbuf[0:SUBLANES, :] = jnp.zeros((SUBLANES, d), F32)
    lev_a[0:SUBLANES, :] = jnp.zeros((SUBLANES, gc), F32)
    lev_b[0:SUBLANES, :] = jnp.zeros((SUBLANES, gc), F32)
    hbuf[SUBLANES:t0, :] = jnp.where(pos0 > 0, hp, 0.0)
    hbuf[t0:t0 + tp, :] = norm(x)
    hbuf[t0 + tp:, :] = jnp.where(pos0 + tp < seq_len, hn, 0.0)

    pos = pos0 + lax.broadcasted_iota(jnp.int32, (tp, 1), 0)
    outs = []
    for gi, w in enumerate(POOL_WINDOWS):
        cols = slice(gi * gc, (gi + 1) * gc)
        src, dst, other = hbuf, lev_a, lev_b
        src_cols = cols
        step = 1
        while step < w:
            cur = src[SUBLANES:SUBLANES + span, src_cols] + src[SUBLANES - step:SUBLANES - step + span, src_cols]
            step *= 2
            if step < w or w > 2:
                dst[SUBLANES:SUBLANES + span, :] = cur
                src, dst, other = dst, other, dst
                src_cols = slice(None)
        if w == 2:
            win = cur[t0 - SUBLANES:t0 - SUBLANES + tp, :]
        else:
            end = t0 + w // 2 - 1
            win = src[end:end + tp, :]
        lo = jnp.maximum(pos - w // 2, 0)
        hi = jnp.minimum(pos + w // 2, seq_len)
        cnt = (hi - lo).astype(F32)
        p = win / cnt - hbuf[t0:t0 + tp, cols]
        outs.append(jnp.dot(p.astype(BF16), w_ref[gi], preferred_element_type=F32))
    o_ref[...] = x + (gt_ref[0] * ps_ref[...]) * jnp.concatenate(outs, axis=-1)


def _pool(x, gain, mod, sub, pool_w, pool_scale, seq, tp=512):
    t, d = x.shape
    hb = tp // SUBLANES
    nhalo = t // SUBLANES
    gc = d // len(POOL_WINDOWS)
    kern = functools.partial(_pool_kernel, tp=tp, seq_len=seq[0])
    rows = POOL_PRE + tp + POOL_POST
    return pl.pallas_call(
        kern,
        grid=(t // tp,),
        in_specs=[
            pl.BlockSpec((tp, d), lambda i: (i, 0)),
            pl.BlockSpec((SUBLANES, d), lambda i: (jnp.maximum(i * hb - 1, 0), 0)),
            pl.BlockSpec((SUBLANES, d), lambda i: (jnp.minimum((i + 1) * hb, nhalo - 1), 0)),
            pl.BlockSpec((1, d), lambda i: (0, 0)),
            *_mod_specs(tp, seq, sub, d, 1),
            pl.BlockSpec(pool_w.shape, lambda i: (0, 0, 0)),
            pl.BlockSpec((1, d), lambda i: (0, 0)),
        ],
        out_specs=pl.BlockSpec((tp, d), lambda i: (i, 0)),
        out_shape=jax.ShapeDtypeStruct((t, d), F32),
        scratch_shapes=[pltpu.VMEM((rows, d), F32), pltpu.VMEM((rows, gc), F32), pltpu.VMEM((rows, gc), F32)],
        compiler_params=_params(("arbitrary",)),
        name="pool_mixer",
    )(x, x, x, gain.reshape(1, d), mod, mod, mod, pool_w, pool_scale.reshape(1, d))


def _rope_tables(s):
    t = jnp.arange(s)
    r = (t // GRID_W).astype(F32)
    c = (t % GRID_W).astype(F32)
    inv = ROPE_THETA ** (-jnp.arange(0, AXIS_DIM, 2, dtype=F32) / AXIS_DIM)
    ang_r = r[:, None] * inv[None, :]
    ang_c = c[:, None] * inv[None, :]
    cos = jnp.concatenate([jnp.cos(ang_r)] * 2 + [jnp.cos(ang_c)] * 2, axis=-1)
    sin = jnp.concatenate([-jnp.sin(ang_r), jnp.sin(ang_r), -jnp.sin(ang_c), jnp.sin(ang_c)], axis=-1)
    return cos, sin


def _head_norm_rope(xh, gain, cos, sin, first_half):
    ms = jnp.mean(xh * xh, axis=-1, keepdims=True)
    y = xh * lax.rsqrt(ms + EPS) * gain
    half = AXIS_DIM // 2
    partner = jnp.where(first_half, pltpu.roll(y, HEAD_DIM - half, 1), pltpu.roll(y, half, 1))
    return y * cos + partner * sin


def _qkv_kernel(x_ref, g_ref, sh_ref, sc_ref, w_ref, qg_ref, kg_ref, cos_ref, sin_ref,
                q_ref, k_ref, v_ref, *, n_heads, n_kv, n_split):
    qg = qg_ref[...] * Q_SCALE
    kg = kg_ref[...]
    k0 = n_heads * HEAD_DIM
    rs = x_ref.shape[0] // n_split
    lane = lax.broadcasted_iota(jnp.int32, (rs, HEAD_DIM), 1)
    first_half = (lane % AXIS_DIM) < (AXIS_DIM // 2)
    for r in range(n_split):
        rows = slice(r * rs, (r + 1) * rs)
        h = _mod_norm(x_ref[rows, :], g_ref[...], sh_ref[0], sc_ref[0]).astype(BF16)
        qkv = jnp.dot(h, w_ref[...], preferred_element_type=F32)
        cos = cos_ref[rows, :]
        sin = sin_ref[rows, :]
        for hd in range(n_heads):
            cols = slice(hd * HEAD_DIM, (hd + 1) * HEAD_DIM)
            q_ref[rows, cols] = _head_norm_rope(qkv[:, cols], qg, cos, sin, first_half).astype(BF16)
        for hd in range(n_kv):
            cols = slice(k0 + hd * HEAD_DIM, k0 + (hd + 1) * HEAD_DIM)
            k_ref[rows, hd * HEAD_DIM:(hd + 1) * HEAD_DIM] = _head_norm_rope(
                qkv[:, cols], kg, cos, sin, first_half).astype(BF16)
        v_ref[rows, :] = qkv[:, k0 + n_kv * HEAD_DIM:].astype(BF16)


def _qkv(x, gain, mod, sub, w_qkv, q_g, k_g, cos, sin, seq, tm=512, n_split=2):
    t, d = x.shape
    n_tot = w_qkv.shape[1] // HEAD_DIM
    n_heads = d // HEAD_DIM
    n_kv = (n_tot - n_heads) // 2
    seq_len = seq[0]
    pos_block = lambda i: (((i * tm) % seq_len) // tm, 0)
    kern = functools.partial(_qkv_kernel, n_heads=n_heads, n_kv=n_kv, n_split=n_split)
    return pl.pallas_call(
        kern,
        grid=(t // tm,),
        in_specs=[
            pl.BlockSpec((tm, d), lambda i: (i, 0)),
            pl.BlockSpec((1, d), lambda i: (0, 0)),
            *_mod_specs(tm, seq, sub, d, 1)[:2],
            pl.BlockSpec(w_qkv.shape, lambda i: (0, 0)),
            pl.BlockSpec((1, HEAD_DIM), lambda i: (0, 0)),
            pl.BlockSpec((1, HEAD_DIM), lambda i: (0, 0)),
            pl.BlockSpec((tm, HEAD_DIM), pos_block),
            pl.BlockSpec((tm, HEAD_DIM), pos_block),
        ],
        out_specs=[
            pl.BlockSpec((tm, n_heads * HEAD_DIM), lambda i: (i, 0)),
            pl.BlockSpec((tm, n_kv * HEAD_DIM), lambda i: (i, 0)),
            pl.BlockSpec((tm, n_kv * HEAD_DIM), lambda i: (i, 0)),
        ],
        out_shape=[
            jax.ShapeDtypeStruct((t, n_heads * HEAD_DIM), BF16),
            jax.ShapeDtypeStruct((t, n_kv * HEAD_DIM), BF16),
            jax.ShapeDtypeStruct((t, n_kv * HEAD_DIM), BF16),
        ],
        compiler_params=_params(("arbitrary",)),
        name="qkv_norm_rope",
    )(x, gain.reshape(1, d), mod, mod, w_qkv, q_g.reshape(1, HEAD_DIM), k_g.reshape(1, HEAD_DIM), cos, sin)


def _flash_kernel(q_ref, *refs, n_row_split):
    o_ref = refs[-1]
    chunks = []
    for c in range(len(refs) // 2):
        vs = refs[2 * c + 1][...]
        chunks.append((refs[2 * c][...], jnp.concatenate([vs, jnp.ones_like(vs)], axis=-1)))
    nblk = chunks[0][0].shape[0] // HEAD_DIM
    rs = q_ref.shape[0] // n_row_split
    for g in range(GQA_GROUP):
        cols = slice(g * HEAD_DIM, (g + 1) * HEAD_DIM)
        for r in range(n_row_split):
            rows = slice(r * rs, (r + 1) * rs)
            q = q_ref[rows, cols]
            m = acc = None
            for ks, v_ext in chunks:
                s = lax.dot_general(q, ks, (((1,), (1,)), ((), ())), preferred_element_type=F32)
                blocks = [s[:, c * HEAD_DIM:(c + 1) * HEAD_DIM] for c in range(nblk)]
                m_cur = jnp.max(functools.reduce(jnp.maximum, blocks), axis=-1, keepdims=True)
                m_new = jnp.broadcast_to(m_cur, blocks[0].shape) if m is None else jnp.maximum(m, m_cur)
                p = jnp.concatenate([jnp.exp2(blk - m_new) for blk in blocks], axis=-1).astype(BF16)
                pv = jnp.dot(p, v_ext, preferred_element_type=F32)
                if m is None:
                    acc = pv
                else:
                    alpha = jnp.exp2(m - m_new)
                    acc = jnp.concatenate([alpha, alpha], axis=-1) * acc + pv
                m = m_new
            o_ref[rows, cols] = (acc[:, :HEAD_DIM] / acc[:, HEAD_DIM:]).astype(o_ref.dtype)


def _flash(q, k, v, seq_len, tq=1024, n_row_split=8):
    t, dq = q.shape
    n_kv = k.shape[1] // HEAD_DIM
    qw = GQA_GROUP * HEAD_DIM
    tk = min(seq_len, KV_CHUNK)
    n_chunks = seq_len // tk
    nq = seq_len // tq
    kv_specs, kv_args = [], []
    for c in range(n_chunks):
        spec = pl.BlockSpec((tk, HEAD_DIM), lambda b, h, i, c=c: (b * n_chunks + c, h))
        kv_specs += [spec, spec]
        kv_args += [k, v]
    kern = functools.partial(_flash_kernel, n_row_split=n_row_split)
    return pl.pallas_call(
        kern,
        grid=(t // seq_len, n_kv, nq),
        in_specs=[pl.BlockSpec((tq, qw), lambda b, h, i: (b * nq + i, h)), *kv_specs],
        out_specs=pl.BlockSpec((tq, qw), lambda b, h, i: (b * nq + i, h)),
        out_shape=jax.ShapeDtypeStruct((t, dq), BF16),
        compiler_params=_params(("arbitrary", "arbitrary", "arbitrary")),
        name="flash_attention",
    )(q, *kv_args)


def _oproj_kernel(x_ref, a_ref, gt_ref, w_ref, o_ref):
    m = jnp.dot(a_ref[...], w_ref[...], preferred_element_type=F32)
    o_ref[...] = x_ref[...] + gt_ref[0] * m


def _oproj(x, attn, mod, sub, w_o, seq, tm=512):
    t, d = x.shape
    return pl.pallas_call(
        _oproj_kernel,
        grid=(t // tm,),
        in_specs=[
            pl.BlockSpec((tm, d), lambda i: (i, 0)),
            pl.BlockSpec((tm, attn.shape[1]), lambda i: (i, 0)),
            _mod_specs(tm, seq, sub, d, 1)[2],
            pl.BlockSpec(w_o.shape, lambda i: (0, 0)),
        ],
        out_specs=pl.BlockSpec((tm, d), lambda i: (i, 0)),
        out_shape=jax.ShapeDtypeStruct((t, d), F32),
        compiler_params=_params(("arbitrary",)),
        name="attn_out_proj",
    )(x, attn, mod, w_o)


def kernel(x_prompt, x_sample, c_prompt, c_sample, ada_w, ada_b, norm_g, ffn_w_in, ffn_w_out,
           pool_w, pool_scale, attn_w_qkv, attn_q_g, attn_k_g, attn_w_o):
    bp, sp, d = x_prompt.shape
    bs, ss, _ = x_sample.shape
    depth = ada_w.shape[0]
    assert sp % GRID_W == 0 and ss % GRID_W == 0
    nseq = bp + bs
    c_all = jnp.concatenate([c_prompt, c_sample], axis=0)
    cos, sin = _rope_tables(max(sp, ss))

    xs = [x_prompt.reshape(bp * sp, d), x_sample.reshape(bs * ss, d)]
    seqs = [(sp, 0), (ss, bp)]
    d_ff = ffn_w_out.shape[2]
    nk = d_ff // FFN_TF
    tiles = [x.shape[0] // FFN_TM for x in xs]

    mods = _ada_mod(c_all, ada_w, ada_b, 0, depth)
    pool_wb = pool_w.astype(BF16)
    w_qkv = {j: attn_w_qkv[j].astype(BF16) for j in range(1, attn_w_qkv.shape[0])}
    w_o = {j: attn_w_o[j].astype(BF16) for j in range(1, attn_w_o.shape[0])}
    w_in_b = ffn_w_in[0, 0].astype(BF16)
    w_out_b = ffn_w_out[0, 0].astype(BF16)

    def ffn_stage(xs, li, fi, w_in_b, w_out_b, mod):
        nli, nfi = divmod(li * 2 + fi + 1, 2)
        jobs = [[] for _ in xs]
        if nli < depth:
            jobs[0].append(("w_in", _cast_job(ffn_w_in, (nli, nfi), tiles[0], nk, True)))
            jobs[1].append(("w_out", _cast_job(ffn_w_out, (nli, nfi), tiles[1], nk, False)))
        if (li, fi) == (0, 0) and depth >= 2:
            jobs[0].append(("w_qkv", _cast_part_job(attn_w_qkv, (0,), tiles[0], nk, SIDE_STEPS)))
            jobs[1].append(("w_o", _cast_part_job(attn_w_o, (0,), tiles[1], nk, SIDE_STEPS)))
        sub = 0 if fi == 0 else 2
        outs, made = [], {}
        for x, seq, jl in zip(xs, seqs, jobs):
            y, extra = _ffn(x, norm_g[li, sub], mod, sub, w_in_b, w_out_b, seq, jobs=[j for _, j in jl])
            outs.append(y)
            made.update({name: e for (name, _), e in zip(jl, extra)})
        return outs, made

    for i in range(depth):
        mod = mods[i].reshape(nseq * N_MOD_ROWS, 1, d)
        j = i // 2
        xs, made = ffn_stage(xs, i, 0, w_in_b, w_out_b, mod)
        w_in_b, w_out_b = made.get("w_in"), made.get("w_out")
        if "w_qkv" in made:
            w_qkv[0], w_o[0] = made["w_qkv"], made["w_o"]
        if i % 2 == 0:
            xs = [_pool(x, norm_g[i, 1], mod, 1, pool_wb[j], pool_scale[j], seq) for x, seq in zip(xs, seqs)]
        else:
            nxt = []
            for x, seq in zip(xs, seqs):
                q, k, v = _qkv(x, norm_g[i, 1], mod, 1, w_qkv[j], attn_q_g[j], attn_k_g[j], cos, sin, seq)
                a = _flash(q, k, v, seq[0])
                nxt.append(_oproj(x, a, mod, 1, w_o[j], seq))
            xs = nxt
        xs, made = ffn_stage(xs, i, 1, w_in_b, w_out_b, mod)
        w_in_b, w_out_b = made.get("w_in"), made.get("w_out")

    return (xs[0].reshape(bp, sp, d), xs[1].reshape(bs, ss, d))
```
